```python
import math
import jax, jax.numpy as jnp
from jax import lax
import numpy as np

D_MODEL = 1024
BATCH = 8
SEQ = 8192
DEPTH = 2
DEC_BATCH = 32
DEC_SEQ = 64
PAST_LEN = 2048

CHUNK = 64
Q_BLOCK = 128
N_DIFF_HEADS = 4
DIFF_DV = 128
DIFF_DK = DIFF_DV // 2
DIFF_WIDTH = N_DIFF_HEADS * DIFF_DV
POOL_WINDOWS = (2, 4, 8, 16)
N_POOL_GROUPS = len(POOL_WINDOWS)
POOL_WIDTH = D_MODEL - DIFF_WIDTH
POOL_GC = POOL_WIDTH // N_POOL_GROUPS
POOL_HIST = max(POOL_WINDOWS) - 1
IN_WIDTH = 3 * DIFF_WIDTH + POOL_WIDTH
N_MEM = 256
N_X_HEADS = 4
X_HEAD_DIM = D_MODEL // N_X_HEADS
D_FF = 4 * D_MODEL
NORM_EPS = 1e-6
SUBLN_EPS = 1e-5

kernel_name = 'hybrid_diffattn_pool_stream_step'


def _rmsnorm(x, g, eps=NORM_EPS):
    xf = x.astype(jnp.float32)
    y = xf * lax.rsqrt(jnp.mean(xf * xf, axis=-1, keepdims=True) + eps)
    return (y * g.astype(jnp.float32)).astype(x.dtype)


def _split_proj(h, w_in):
    B, S, _ = h.shape
    z = h @ w_in
    q = z[..., :DIFF_WIDTH].reshape(B, S, 2, N_DIFF_HEADS, DIFF_DK)
    k = z[..., DIFF_WIDTH:2 * DIFF_WIDTH].reshape(B, S, 2, N_DIFF_HEADS, DIFF_DK)
    v = z[..., 2 * DIFF_WIDTH:3 * DIFF_WIDTH].reshape(B, S, N_DIFF_HEADS, DIFF_DV)
    u = z[..., 3 * DIFF_WIDTH:]
    return q, k, v, u


def _lambda(lam_q, lam_k, lam_init):
    lf = jnp.sum(lam_q.astype(jnp.float32) * lam_k.astype(jnp.float32), axis=-1)
    return jnp.exp(lf[0]) - jnp.exp(lf[1]) + lam_init


def _chunk_mask(q_pos, k_pos):
    return (k_pos[None, :] // CHUNK) <= (q_pos[:, None] // CHUNK)


def _diff_attention(q, k, v, q_pos, k_pos, lam):
    s = jnp.einsum('bqmhd,bkmhd->bmhqk', q, k).astype(jnp.float32) * (DIFF_DK ** -0.5)
    s = jnp.where(_chunk_mask(q_pos, k_pos), s, -jnp.inf)
    p = jax.nn.softmax(s, axis=-1)
    a = (p[:, 0] - lam * p[:, 1]).astype(v.dtype)
    return jnp.einsum('bhqk,bkhd->bqhd', a, v)


def _blocked_diff_attention(q, k, v, lam):
    B, S = q.shape[:2]
    nb = S // Q_BLOCK
    qb = jnp.moveaxis(q.reshape(B, nb, Q_BLOCK, 2, N_DIFF_HEADS, DIFF_DK), 1, 0)
    k_pos = jnp.arange(S)

    def one(args):
        q_blk, i = args
        q_pos = i * Q_BLOCK + jnp.arange(Q_BLOCK)
        return _diff_attention(q_blk, k, v, q_pos, k_pos, lam)

    o = lax.map(one, (qb, jnp.arange(nb)))
    return jnp.moveaxis(o, 0, 1).reshape(B, S, N_DIFF_HEADS, DIFF_DV)


def _diff_post(o, subln_g, lam_init):
    B, S = o.shape[:2]
    return (_rmsnorm(o, subln_g, SUBLN_EPS) * (1.0 - lam_init)).reshape(B, S, DIFF_WIDTH)


def _pool_branch(u, hist, start, w_pool, pool_scale):
    B, S, P = u.shape
    ext = jnp.concatenate([hist, u], axis=1)
    cs = jnp.cumsum(ext.astype(jnp.float32), axis=1)
    cs = jnp.pad(cs, ((0, 0), (1, 0), (0, 0)))
    pos = start + jnp.arange(S)
    means = []
    for g, w in enumerate(POOL_WINDOWS):
        sl = slice(g * POOL_GC, (g + 1) * POOL_GC)
        win = cs[:, POOL_HIST + 1:POOL_HIST + 1 + S, sl] - cs[:, POOL_HIST + 1 - w:POOL_HIST + 1 - w + S, sl]
        cnt = jnp.minimum(w, pos + 1).astype(jnp.float32)[None, :, None]
        means.append(win / cnt)
    mean = jnp.stack(means, axis=2)
    pooled = (mean - u.astype(jnp.float32).reshape(B, S, N_POOL_GROUPS, POOL_GC)).astype(u.dtype)
    y = jnp.einsum('bsgc,gcd->bsgd', pooled, w_pool).reshape(B, S, P) * pool_scale
    return y, ext[:, -POOL_HIST:]


def _mem_kv(mem, g, wk, wv):
    B = mem.shape[0]
    m = _rmsnorm(mem, g)
    mk = (m @ wk).reshape(B, N_MEM, N_X_HEADS, X_HEAD_DIM)
    mv = (m @ wv).reshape(B, N_MEM, N_X_HEADS, X_HEAD_DIM)
    return mk, mv


def _cross_attn(h, mk, mv, wq, wo):
    B, S, _ = h.shape
    q = (h @ wq).reshape(B, S, N_X_HEADS, X_HEAD_DIM)
    s = jnp.einsum('bshd,bmhd->bhsm', q, mk).astype(jnp.float32) * (X_HEAD_DIM ** -0.5)
    p = jax.nn.softmax(s, axis=-1).astype(mv.dtype)
    o = jnp.einsum('bhsm,bmhd->bshd', p, mv).reshape(B, S, D_MODEL)
    return o @ wo


def _mlp(h, w_up, w_down):
    return jnp.square(jax.nn.relu(h @ w_up)) @ w_down


def _rest(x, a, pool_y, w_out, mk, mv, norm_x_g, wq_x, wo_x, norm_mlp_g, w_up, w_down):
    x = x + jnp.concatenate([a, pool_y], axis=-1) @ w_out
    x = x + _cross_attn(_rmsnorm(x, norm_x_g), mk, mv, wq_x, wo_x)
    return x + _mlp(_rmsnorm(x, norm_mlp_g), w_up, w_down)


def setup_inputs(seed: int = 0) -> dict:
    key = jax.random.key(seed)
    ks = iter(jax.random.split(key, 32))
    f32 = jnp.float32

    def nrm(shape, scale=1.0):
        return jax.random.normal(next(ks), shape, f32) * scale

    def gain(shape):
        return 1.0 + 0.05 * jax.random.normal(next(ks), shape, f32)

    return {
        'x_prompt': nrm((BATCH, SEQ, D_MODEL)),
        'x_sample': nrm((DEC_BATCH, DEC_SEQ, D_MODEL)),
        'cache_k': nrm((DEPTH, DEC_BATCH, PAST_LEN, 2, N_DIFF_HEADS, DIFF_DK)),
        'cache_v': nrm((DEPTH, DEC_BATCH, PAST_LEN, N_DIFF_HEADS, DIFF_DV)),
        'state_pool': nrm((DEPTH, DEC_BATCH, POOL_HIST, POOL_WIDTH)),
        'cache_mem_k': nrm((DEPTH, DEC_BATCH, N_MEM, N_X_HEADS, X_HEAD_DIM)),
        'cache_mem_v': nrm((DEPTH, DEC_BATCH, N_MEM, N_X_HEADS, X_HEAD_DIM)),
        'mem_prompt': nrm((BATCH, N_MEM, D_MODEL)),
        'norm_mix_g': gain((DEPTH, D_MODEL)),
        'w_in': nrm((DEPTH, D_MODEL, IN_WIDTH), D_MODEL ** -0.5),
        'lam_q': nrm((DEPTH, 2, DIFF_DK), 0.1),
        'lam_k': nrm((DEPTH, 2, DIFF_DK), 0.1),
        'subln_g': gain((DEPTH, DIFF_DV)),
        'w_pool': nrm((DEPTH, N_POOL_GROUPS, POOL_GC, POOL_GC), POOL_GC ** -0.5),
        'pool_scale': gain((DEPTH, POOL_WIDTH)),
        'w_out': nrm((DEPTH, D_MODEL, D_MODEL), D_MODEL ** -0.5),
        'norm_x_g': gain((DEPTH, D_MODEL)),
        'norm_mem_g': gain((DEPTH, D_MODEL)),
        'wq_x': nrm((DEPTH, D_MODEL, D_MODEL), D_MODEL ** -0.5),
        'wk_x': nrm((DEPTH, D_MODEL, D_MODEL), D_MODEL ** -0.5),
        'wv_x': nrm((DEPTH, D_MODEL, D_MODEL), D_MODEL ** -0.5),
        'wo_x': nrm((DEPTH, D_MODEL, D_MODEL), D_MODEL ** -0.5),
        'norm_mlp_g': gain((DEPTH, D_MODEL)),
        'w_up': nrm((DEPTH, D_MODEL, D_FF), D_MODEL ** -0.5),
        'w_down': nrm((DEPTH, D_FF, D_MODEL), D_FF ** -0.5),
        'final_g': gain((D_MODEL,)),
    }


def reference(x_prompt, x_sample, cache_k, cache_v, state_pool, cache_mem_k, cache_mem_v, mem_prompt,
              norm_mix_g, w_in, lam_q, lam_k, subln_g, w_pool, pool_scale, w_out,
              norm_x_g, norm_mem_g, wq_x, wk_x, wv_x, wo_x, norm_mlp_g, w_up, w_down, final_g):
    xp, xs = x_prompt, x_sample
    past = cache_k.shape[2]
    n_new = xs.shape[1]
    q_pos_s = past + jnp.arange(n_new)
    k_pos_s = jnp.arange(past + n_new)
    zero_hist = jnp.zeros((xp.shape[0], POOL_HIST, POOL_WIDTH), xp.dtype)
    kp_l, vp_l, pp_l, mkp_l, mvp_l, ks_l, vs_l, ps_l = [], [], [], [], [], [], [], []
    for l in range(DEPTH):
        lam_init = 0.8 - 0.6 * math.exp(-0.3 * l)
        lam = _lambda(lam_q[l], lam_k[l], lam_init)

        q, k, v, u = _split_proj(_rmsnorm(xp, norm_mix_g[l]), w_in[l])
        a = _diff_post(_blocked_diff_attention(q, k, v, lam), subln_g[l], lam_init)
        pool_y, hist_p = _pool_branch(u, zero_hist, 0, w_pool[l], pool_scale[l])
        mk, mv = _mem_kv(mem_prompt, norm_mem_g[l], wk_x[l], wv_x[l])
        xp = _rest(xp, a, pool_y, w_out[l], mk, mv, norm_x_g[l], wq_x[l], wo_x[l],
                   norm_mlp_g[l], w_up[l], w_down[l])
        kp_l.append(k); vp_l.append(v); pp_l.append(hist_p); mkp_l.append(mk); mvp_l.append(mv)

        q, k, v, u = _split_proj(_rmsnorm(xs, norm_mix_g[l]), w_in[l])
        k_all = jnp.concatenate([cache_k[l], k], axis=1)
        v_all = jnp.concatenate([cache_v[l], v], axis=1)
        a = _diff_post(_diff_attention(q, k_all, v_all, q_pos_s, k_pos_s, lam), subln_g[l], lam_init)
        pool_y, hist_s = _pool_branch(u, state_pool[l], past, w_pool[l], pool_scale[l])
        xs = _rest(xs, a, pool_y, w_out[l], cache_mem_k[l], cache_mem_v[l], norm_x_g[l], wq_x[l], wo_x[l],
                   norm_mlp_g[l], w_up[l], w_down[l])
        ks_l.append(k); vs_l.append(v); ps_l.append(hist_s)

    y_prompt = _rmsnorm(xp, final_g)
    y_sample = _rmsnorm(xs, final_g)
    return (y_prompt, y_sample,
            jnp.stack(kp_l), jnp.stack(vp_l), jnp.stack(pp_l), jnp.stack(mkp_l), jnp.stack(mvp_l),
            jnp.stack(ks_l), jnp.stack(vs_l), jnp.stack(ps_l))
```

```python
import functools
import math

import jax
import jax.numpy as jnp
from jax import lax
from jax.experimental import pallas as pl
from jax.experimental.pallas import tpu as pltpu

F32 = jnp.float32
BF16 = jnp.bfloat16

CHUNK = 64
CHUNK_SHIFT = CHUNK.bit_length() - 1
N_DIFF_HEADS = 4
DIFF_DK = 64
DIFF_DV = 2 * DIFF_DK
DIFF_WIDTH = N_DIFF_HEADS * DIFF_DV
POOL_WINDOWS = (2, 4, 8, 16)
POOL_GC = 128
POOL_WIDTH = len(POOL_WINDOWS) * POOL_GC
POOL_HIST = max(POOL_WINDOWS) - 1
HIST_ROWS = 16
N_X_HEADS = 4
NORM_EPS = 1e-6
SUBLN_EPS = 1e-5
LOG2E = 1.4426950408889634

ROW_TILE = 512
ATTN_TILE = 512
MLP_CHUNKS = 4
V7X_VMEM_LIMIT = 56 * 2**20

_NT = (((1,), (1,)), ((), ()))


def _rms_bf16(x, g, eps):
    ms = jnp.mean(x * x, axis=-1, keepdims=True)
    return (x * lax.rsqrt(ms + eps) * g).astype(BF16)


def _row_tiling(batch, seq):
    if seq >= ROW_TILE:
        assert seq % ROW_TILE == 0
        return 1, ROW_TILE
    bb = max(1, min(batch, ROW_TILE // seq))
    assert batch % bb == 0 and seq % 8 == 0
    return bb, seq


def _params(*semantics):
    return pltpu.CompilerParams(dimension_semantics=semantics, vmem_limit_bytes=V7X_VMEM_LIMIT)


def _const_spec(shape):
    nd = len(shape)
    return pl.BlockSpec(shape, lambda *_: (0,) * nd, pipeline_mode=pl.Buffered(1))


def _in_proj_kernel(x_ref, g_ref, w_ref, hist_ref, wp_ref, ps_ref,
                    q_ref, ka_ref, va_ref, ko_ref, vo_ref, py_ref, ph_ref,
                    ext_ref, *, bb, ts, start):
    s = pl.program_id(1)
    d = x_ref.shape[-1]
    m = bb * ts
    w = DIFF_WIDTH
    h = _rms_bf16(x_ref[...].reshape(m, d), g_ref[...], NORM_EPS)
    z = jnp.dot(h, w_ref[...], preferred_element_type=F32)
    q_ref[...] = z[:, 0:w].astype(BF16).reshape(bb, ts, w)
    ka_ref[...] = z[:, w:2 * w].astype(BF16).reshape(bb, ts, w)
    v = z[:, 2 * w:3 * w]
    va_ref[...] = v.astype(BF16).reshape(bb, ts, w)
    vo_ref[...] = v.reshape(bb, ts, w)
    ko_ref[...] = z[:, 3 * w:4 * w].reshape(bb, ts, w)

    @pl.when(s == 0)
    def _():
        ext_ref[:, 0:HIST_ROWS, :] = hist_ref[...]

    ext_ref[:, HIST_ROWS:HIST_ROWS + ts, :] = z[:, 4 * w:].reshape(bb, ts, POOL_WIDTH)
    pos = start + s * ts + lax.broadcasted_iota(jnp.int32, (ts, 1), 0)
    for gi, win_len in enumerate(POOL_WINDOWS):
        cols = slice(gi * POOL_GC, (gi + 1) * POOL_GC)
        inv_cnt = 1.0 / jnp.minimum(win_len, pos + 1).astype(F32)
        pooled = []
        for i in range(bb):
            u = ext_ref[i, HIST_ROWS:HIST_ROWS + ts, cols]
            win = u
            for k in range(1, win_len):
                win = win + ext_ref[i, HIST_ROWS - k:HIST_ROWS - k + ts, cols]
            pooled.append((win * inv_cnt - u).astype(BF16))
        pooled = pooled[0] if bb == 1 else jnp.concatenate(pooled, axis=0)
        y = jnp.dot(pooled, wp_ref[gi], preferred_element_type=F32) * ps_ref[:, cols]
        py_ref[:, :, cols] = y.astype(BF16).reshape(bb, ts, POOL_GC)
    tail = ext_ref[:, ts:ts + HIST_ROWS, :]
    ph_ref[...] = tail
    ext_ref[:, 0:HIST_ROWS, :] = tail


def _in_proj(x, g, w_aug, hist, w_pool, pool_scale, start):
    batch, seq, d = x.shape
    bb, ts = _row_tiling(batch, seq)
    assert seq >= HIST_ROWS
    w = DIFF_WIDTH
    tok = lambda width: pl.BlockSpec((bb, ts, width), lambda b, s: (b, s, 0))
    out_shape = [jax.ShapeDtypeStruct((batch, seq, w), BF16)] * 3 + \
                [jax.ShapeDtypeStruct((batch, seq, w), F32)] * 2 + \
                [jax.ShapeDtypeStruct((batch, seq, POOL_WIDTH), BF16),
                 jax.ShapeDtypeStruct((batch, HIST_ROWS, POOL_WIDTH), F32)]
    hist_spec = pl.BlockSpec((bb, HIST_ROWS, POOL_WIDTH), lambda b, s: (b, 0, 0))
    return pl.pallas_call(
        functools.partial(_in_proj_kernel, bb=bb, ts=ts, start=start),
        grid=(batch // bb, seq // ts),
        in_specs=[tok(d), _const_spec(g.shape), _const_spec(w_aug.shape), hist_spec,
                  _const_spec(w_pool.shape), _const_spec(pool_scale.shape)],
        out_specs=[tok(w)] * 5 + [tok(POOL_WIDTH), hist_spec],
        out_shape=out_shape,
        scratch_shapes=[pltpu.VMEM((bb, HIST_ROWS + ts, POOL_WIDTH), F32)],
        compiler_params=_params("arbitrary", "arbitrary"),
        name="in_proj",
    )(x, g, w_aug, hist, w_pool, pool_scale)


def _lambda(lq_ref, lk_ref, lam_init):
    e = jnp.exp(jnp.sum(lq_ref[...] * lk_ref[...], axis=-1, keepdims=True))
    return e[0:1] - e[1:2] + lam_init


def _diff_post(o0, o1, lam, sg, lam_init):
    o = o0 - lam * o1
    ms = jnp.mean(o * o, axis=-1, keepdims=True)
    return (o * lax.rsqrt(ms + SUBLN_EPS) * sg * (1.0 - lam_init)).astype(BF16)


def _prompt_attn_kernel(q_ref, k_ref, v_ref, lq_ref, lk_ref, sg_ref, o_ref, *, t, lam_init):
    i = pl.program_id(2)
    q = q_ref[...]
    lane = lax.broadcasted_iota(jnp.int32, q.shape, 1)
    zero = jnp.zeros_like(q)
    q2 = jnp.concatenate([jnp.where(lane < DIFF_DK, q, zero), jnp.where(lane >= DIFF_DK, q, zero)], axis=0)

    off = pl.multiple_of(i * t, t)
    s = lax.dot_general(q2, k_ref[pl.ds(off, t), :], _NT, preferred_element_type=F32)
    row = lax.broadcasted_iota(jnp.int32, s.shape, 0)
    col = lax.broadcasted_iota(jnp.int32, s.shape, 1)
    q_chunk = lax.shift_right_logical(jnp.bitwise_and(row, t - 1), CHUNK_SHIFT)
    s = jnp.where(lax.shift_right_logical(col, CHUNK_SHIFT) <= q_chunk, s, -jnp.inf)
    m = jnp.max(s, axis=-1, keepdims=True)
    p = jnp.exp2(s - m)
    l = jnp.sum(p, axis=-1, keepdims=True)
    acc = jnp.dot(p.astype(BF16), v_ref[pl.ds(off, t), :], preferred_element_type=F32)

    def body(j, carry):
        m, l, acc = carry
        joff = pl.multiple_of(j * t, t)
        s = lax.dot_general(q2, k_ref[pl.ds(joff, t), :], _NT, preferred_element_type=F32)
        m_new = jnp.maximum(m, jnp.max(s, axis=-1, keepdims=True))
        alpha = jnp.exp2(m - m_new)
        p = jnp.exp2(s - m_new)
        l = alpha * l + jnp.sum(p, axis=-1, keepdims=True)
        acc = alpha * acc + jnp.dot(p.astype(BF16), v_ref[pl.ds(joff, t), :], preferred_element_type=F32)
        return m_new, l, acc

    m, l, acc = lax.fori_loop(0, i, body, (m, l, acc))
    o = acc * (1.0 / l)
    lam = _lambda(lq_ref, lk_ref, lam_init)
    o_ref[...] = _diff_post(o[:t], o[t:], lam, sg_ref[...], lam_init)


def _prompt_attn(q, k, v, lam_q, lam_k, subln_g, lam_init):
    batch, seq, _ = q.shape
    t = min(ATTN_TILE, seq)
    assert seq % t == 0 and t % CHUNK == 0 and (t & (t - 1)) == 0
    blk = pl.BlockSpec((None, t, DIFF_DV), lambda b, h, i: (b, i, h))
    full = pl.BlockSpec((None, seq, DIFF_DV), lambda b, h, i: (b, 0, h))
    return pl.pallas_call(
        functools.partial(_prompt_attn_kernel, t=t, lam_init=lam_init),
        grid=(batch, N_DIFF_HEADS, seq // t),
        in_specs=[blk, full, full, _const_spec(lam_q.shape), _const_spec(lam_k.shape), _const_spec(subln_g.shape)],
        out_specs=blk,
        out_shape=jax.ShapeDtypeStruct((batch, seq, DIFF_WIDTH), BF16),
        compiler_params=_params("arbitrary", "arbitrary", "arbitrary"),
        name="prompt_attn",
    )(q, k, v, lam_q, lam_k, subln_g)


def _sample_attn_kernel(q_ref, kn_ref, vn_ref, ck_ref, cv_ref, lq_ref, lk_ref, sg_ref, o_ref,
                        *, past, n_new, lam_init):
    lam = _lambda(lq_ref, lk_ref, lam_init)
    sg = sg_ref[...]
    lane = lax.broadcasted_iota(jnp.int32, (n_new, DIFF_DV), 1)
    r = lax.broadcasted_iota(jnp.int32, (n_new, n_new), 0)
    c = lax.broadcasted_iota(jnp.int32, (n_new, n_new), 1)
    visible = lax.shift_right_logical(past + c, CHUNK_SHIFT) <= lax.shift_right_logical(past + r, CHUNK_SHIFT)
    for h in range(N_DIFF_HEADS):
        hs = slice(h * DIFF_DV, (h + 1) * DIFF_DV)
        q_blk = q_ref[:, hs].astype(F32)
        v_c = cv_ref[:, hs].astype(BF16)
        v_n = vn_ref[:, hs].astype(BF16)
        half = h % 2
        o_maps = []
        for mp in range(2):
            qm = q_blk if mp == half else pltpu.roll(q_blk, DIFF_DK, axis=1)
            qm = jnp.where((lane >= DIFF_DK) == bool(half), qm, 0.0).astype(BF16)
            ks = slice((2 * mp + h // 2) * DIFF_DV, (2 * mp + h // 2 + 1) * DIFF_DV)
            s_c = lax.dot_general(qm, ck_ref[:, ks].astype(BF16), _NT, preferred_element_type=F32)
            s_n = lax.dot_general(qm, kn_ref[:, ks].astype(BF16), _NT, preferred_element_type=F32)
            s_n = jnp.where(visible, s_n, -jnp.inf)
            mx = jnp.maximum(jnp.max(s_c, axis=-1, keepdims=True), jnp.max(s_n, axis=-1, keepdims=True))
            p_c = jnp.exp2(s_c - mx)
            p_n = jnp.exp2(s_n - mx)
            l = jnp.sum(p_c, axis=-1, keepdims=True) + jnp.sum(p_n, axis=-1, keepdims=True)
            o = jnp.dot(p_c.astype(BF16), v_c, preferred_element_type=F32) + \
                jnp.dot(p_n.astype(BF16), v_n, preferred_element_type=F32)
            o_maps.append(o * (1.0 / l))
        o_ref[:, hs] = _diff_post(o_maps[0], o_maps[1], lam, sg, lam_init)


def _sample_attn(q, k_new, v_new, cache_k, cache_v, layer, lam_q, lam_k, subln_g, lam_init):
    batch, n_new, _ = q.shape
    past = cache_k.shape[2]
    w = DIFF_WIDTH
    new = pl.BlockSpec((None, n_new, w), lambda b: (b, 0, 0))
    cached = pl.BlockSpec((None, None, past, w), lambda b: (layer, b, 0, 0))
    return pl.pallas_call(
        functools.partial(_sample_attn_kernel, past=past, n_new=n_new, lam_init=lam_init),
        grid=(batch,),
        in_specs=[new, new, new, cached, cached,
                  _const_spec(lam_q.shape), _const_spec(lam_k.shape), _const_spec(subln_g.shape)],
        out_specs=new,
        out_shape=jax.ShapeDtypeStruct((batch, n_new, w), BF16),
        compiler_params=_params("arbitrary"),
        name="sample_attn",
    )(q, k_new, v_new, cache_k, cache_v, lam_q, lam_k, subln_g)


def _mem_kv_kernel(m_ref, g_ref, w_ref, k_ref, v_ref, kb_ref, vb_ref):
    d = m_ref.shape[-1]
    z = jnp.dot(_rms_bf16(m_ref[...], g_ref[...], NORM_EPS), w_ref[...], preferred_element_type=F32)
    k_ref[...] = z[:, :d]
    v_ref[...] = z[:, d:]
    kb_ref[...] = z[:, :d].astype(BF16)
    vb_ref[...] = z[:, d:].astype(BF16)


def _mem_kv(mem, g, w_kv):
    batch, n_mem, d = mem.shape
    rows = batch * n_mem
    tm = min(ROW_TILE, rows)
    assert rows % tm == 0
    tok = pl.BlockSpec((tm, d), lambda i: (i, 0))
    outs = pl.pallas_call(
        _mem_kv_kernel,
        grid=(rows // tm,),
        in_specs=[tok, _const_spec(g.shape), _const_spec(w_kv.shape)],
        out_specs=[tok] * 4,
        out_shape=[jax.ShapeDtypeStruct((rows, d), F32)] * 2 + [jax.ShapeDtypeStruct((rows, d), BF16)] * 2,
        compiler_params=_params("arbitrary"),
        name="mem_kv",
    )(mem.reshape(rows, d), g, w_kv)
    return [o.reshape(batch, n_mem, d) for o in outs]


def _mix_kernel(x_ref, a_ref, p_ref, wo_ref, g_ref, wq_ref, mk_ref, mv_ref, wox_ref, o_ref, *, bb, ts):
    d = x_ref.shape[-1]
    m = bb * ts
    hd = d // N_X_HEADS
    ap = jnp.concatenate([a_ref[...].reshape(m, -1), p_ref[...].reshape(m, -1)], axis=-1)
    x1 = x_ref[...].reshape(m, d) + jnp.dot(ap, wo_ref[...], preferred_element_type=F32)
    h = _rms_bf16(x1, g_ref[...], NORM_EPS)
    q = jnp.dot(h, wq_ref[...], preferred_element_type=F32).astype(BF16)
    rows = []
    for i in range(bb):
        heads = []
        for hh in range(N_X_HEADS):
            cs = slice(hh * hd, (hh + 1) * hd)
            s = lax.dot_general(q[i * ts:(i + 1) * ts, cs], mk_ref[i, :, cs], _NT, preferred_element_type=F32)
            p = jnp.exp(s - jnp.max(s, axis=-1, keepdims=True))
            inv_l = 1.0 / jnp.sum(p, axis=-1, keepdims=True)
            o = jnp.dot(p.astype(BF16), mv_ref[i, :, cs], preferred_element_type=F32) * inv_l
            heads.append(o.astype(BF16))
        rows.append(jnp.concatenate(heads, axis=-1))
    o_all = rows[0] if bb == 1 else jnp.concatenate(rows, axis=0)
    o_ref[...] = (x1 + jnp.dot(o_all, wox_ref[...], preferred_element_type=F32)).reshape(bb, ts, d)


def _mix(x, a, pool_y, w_out, g, wq, mk, mv, wo_x):
    batch, seq, d = x.shape
    bb, ts = _row_tiling(batch, seq)
    n_mem = mk.shape[1]
    tok = lambda width: pl.BlockSpec((bb, ts, width), lambda b, s: (b, s, 0))
    mem = pl.BlockSpec((bb, n_mem, d), lambda b, s: (b, 0, 0))
    return pl.pallas_call(
        functools.partial(_mix_kernel, bb=bb, ts=ts),
        grid=(batch // bb, seq // ts),
        in_specs=[tok(d), tok(a.shape[-1]), tok(pool_y.shape[-1]), _const_spec(w_out.shape), _const_spec(g.shape),
                  _const_spec(wq.shape), mem, mem, _const_spec(wo_x.shape)],
        out_specs=tok(d),
        out_shape=jax.ShapeDtypeStruct(x.shape, F32),
        compiler_params=_params("arbitrary", "arbitrary"),
        name="mix",
    )(x, a, pool_y, w_out, g, wq, mk, mv, wo_x)


def _mlp_kernel(x_ref, g_ref, wu_ref, wd_ref, fg_ref, o_ref, *, bb, ts, final):
    d = x_ref.shape[-1]
    m = bb * ts
    x = x_ref[...].reshape(m, d)
    h = _rms_bf16(x, g_ref[...], NORM_EPS)
    ck = wu_ref.shape[1] // MLP_CHUNKS
    acc = x
    for j in range(MLP_CHUNKS):
        up = jnp.maximum(jnp.dot(h, wu_ref[:, j * ck:(j + 1) * ck], preferred_element_type=F32), 0.0)
        acc = acc + jnp.dot((up * up).astype(BF16), wd_ref[j * ck:(j + 1) * ck, :], preferred_element_type=F32)
    if final:
        ms = jnp.mean(acc * acc, axis=-1, keepdims=True)
        acc = acc * lax.rsqrt(ms + NORM_EPS) * fg_ref[...]
    o_ref[...] = acc.reshape(bb, ts, d)


def _mlp(x, g, w_up, w_down, final_g, final):
    batch, seq, d = x.shape
    bb, ts = _row_tiling(batch, seq)
    assert w_up.shape[1] % MLP_CHUNKS == 0
    tok = pl.BlockSpec((bb, ts, d), lambda b, s: (b, s, 0))
    return pl.pallas_call(
        functools.partial(_mlp_kernel, bb=bb, ts=ts, final=final),
        grid=(batch // bb, seq // ts),
        in_specs=[tok, _const_spec(g.shape), _const_spec(w_up.shape), _const_spec(w_down.shape),
                  _const_spec(final_g.shape)],
        out_specs=tok,
        out_shape=jax.ShapeDtypeStruct(x.shape, F32),
        compiler_params=_params("arbitrary", "arbitrary"),
        name="mlp",
    )(x, g, w_up, w_down, final_g)


def _head_major(w):
    d = w.shape[0]
    return w.reshape(d, 2, N_DIFF_HEADS, DIFF_DK).transpose(0, 2, 1, 3).reshape(d, DIFF_WIDTH)


def _in_weights(w_in):
    w = DIFF_WIDTH
    wq, wk, wv, wu = w_in[:, :w], w_in[:, w:2 * w], w_in[:, 2 * w:3 * w], w_in[:, 3 * w:]
    wq = _head_major(wq) * (DIFF_DK ** -0.5 * LOG2E)
    return jnp.concatenate([wq, _head_major(wk), wv, wk, wu], axis=1).astype(BF16)


def kernel(x_prompt, x_sample, cache_k, cache_v, state_pool, cache_mem_k, cache_mem_v, mem_prompt,
           norm_mix_g, w_in, lam_q, lam_k, subln_g, w_pool, pool_scale, w_out,
           norm_x_g, norm_mem_g, wq_x, wk_x, wv_x, wo_x, norm_mlp_g, w_up, w_down, final_g):
    depth = w_in.shape[0]
    batch, seq, d = x_prompt.shape
    dec_batch, dec_seq, _ = x_sample.shape
    past = cache_k.shape[2]
    n_mem = mem_prompt.shape[1]
    hd = d // N_X_HEADS

    ck = cache_k.reshape(depth, dec_batch, past, DIFF_WIDTH)
    cv = cache_v.reshape(depth, dec_batch, past, DIFF_WIDTH)
    cmk = cache_mem_k.reshape(depth, dec_batch, n_mem, d).astype(BF16)
    cmv = cache_mem_v.reshape(depth, dec_batch, n_mem, d).astype(BF16)
    hist_s = jnp.pad(state_pool, ((0, 0), (0, 0), (HIST_ROWS - POOL_HIST, 0), (0, 0)))
    hist_p = jnp.zeros((batch, HIST_ROWS, POOL_WIDTH), F32)
    row = lambda v: v.reshape(1, -1)
    fg = row(final_g)

    xp, xs = x_prompt, x_sample
    outs = [[] for _ in range(8)]
    for l in range(depth):
        lam_init = 0.8 - 0.6 * math.exp(-0.3 * l)
        last = l == depth - 1
        w_aug = _in_weights(w_in[l])
        wp = w_pool[l].astype(BF16)
        ps = row(pool_scale[l])
        g_mix, g_x, g_mlp, sg = row(norm_mix_g[l]), row(norm_x_g[l]), row(norm_mlp_g[l]), row(subln_g[l])
        w_o = w_out[l].astype(BF16)
        wq = (wq_x[l] * hd ** -0.5).astype(BF16)
        wo = wo_x[l].astype(BF16)
        wu, wd = w_up[l].astype(BF16), w_down[l].astype(BF16)
        w_kv = jnp.concatenate([wk_x[l], wv_x[l]], axis=1).astype(BF16)

        q, ka, va, ko, vo, py, ph = _in_proj(xp, g_mix, w_aug, hist_p, wp, ps, 0)
        a = _prompt_attn(q, ka, va, lam_q[l], lam_k[l], sg, lam_init)
        mk, mv, mkb, mvb = _mem_kv(mem_prompt, row(norm_mem_g[l]), w_kv)
        xp = _mix(xp, a, py, w_o, g_x, wq, mkb, mvb, wo)
        xp = _mlp(xp, g_mlp, wu, wd, fg, last)
        for dst, val in zip(outs[:5], (ko, vo, ph[:, HIST_ROWS - POOL_HIST:], mk, mv)):
            dst.append(val)

        q, _, _, ko, vo, py, ph = _in_proj(xs, g_mix, w_aug, hist_s[l], wp, ps, past)
        a = _sample_attn(q, ko, vo, ck, cv, l, lam_q[l], lam_k[l], sg, lam_init)
        xs = _mix(xs, a, py, w_o, g_x, wq, cmk[l], cmv[l], wo)
        xs = _mlp(xs, g_mlp, wu, wd, fg, last)
        for dst, val in zip(outs[5:], (ko, vo, ph[:, HIST_ROWS - POOL_HIST:])):
            dst.append(val)

    kp, vp, pp, mkp, mvp, ks, vs, pps = [jnp.stack(o) for o in outs]
    return (xp, xs,
            kp.reshape(depth, batch, seq, 2, N_DIFF_HEADS, DIFF_DK),
            vp.reshape(depth, batch, seq, N_DIFF_HEADS, DIFF_DV),
            pp,
            mkp.reshape(depth, batch, n_mem, N_X_HEADS, hd),
            mvp.reshape(depth, batch, n_mem, N_X_HEADS, hd),
            ks.reshape(depth, dec_batch, dec_seq, 2, N_DIFF_HEADS, DIFF_DK),
            vs.reshape(depth, dec_batch, dec_seq, N_DIFF_HEADS, DIFF_DV),
            pps)
```

```python
import functools
import math

import jax
import jax.numpy as jnp
from jax import lax
from jax.experimental import pallas as pl
from jax.experimental.pallas import tpu as pltpu

F32 = jnp.float32
BF16 = jnp.bfloat16

CHUNK = 64
CHUNK_SHIFT = CHUNK.bit_length() - 1
N_DIFF_HEADS = 4
DIFF_DK = 64
DIFF_DV = 2 * DIFF_DK
DIFF_WIDTH = N_DIFF_HEADS * DIFF_DV
POOL_WINDOWS = (2, 4, 8, 16)
POOL_GC = 128
POOL_WIDTH = len(POOL_WINDOWS) * POOL_GC
POOL_HIST = max(POOL_WINDOWS) - 1
HIST_ROWS = 16
N_X_HEADS = 4
NORM_EPS = 1e-6
SUBLN_EPS = 1e-5
LOG2E = 1.4426950408889634

ROW_TILE = 512
MLP_CHUNKS = 4
V7X_VMEM_LIMIT = 56 * 2**20

_NT = (((1,), (1,)), ((), ()))


def _rms_bf16(x, g, eps):
    ms = jnp.mean(x * x, axis=-1, keepdims=True)
    return (x * lax.rsqrt(ms + eps) * g).astype(BF16)


def _row_tiling(batch, seq):
    if seq >= ROW_TILE:
        assert seq % ROW_TILE == 0
        return 1, ROW_TILE
    bb = max(1, min(batch, ROW_TILE // seq))
    assert batch % bb == 0 and seq % 8 == 0
    return bb, seq


def _params(*semantics):
    return pltpu.CompilerParams(dimension_semantics=semantics, vmem_limit_bytes=V7X_VMEM_LIMIT)


def _const_spec(shape):
    nd = len(shape)
    return pl.BlockSpec(shape, lambda *_: (0,) * nd, pipeline_mode=pl.Buffered(1))


def _proj_and_pool(x_ref, g_ref, w_ref, hist_ref, wp_ref, ps_ref, ko_ref, vo_ref, py_ref, ph_ref, ext_ref,
                   bb, ts, start):
    s = pl.program_id(1)
    d = x_ref.shape[-1]
    m = bb * ts
    w = DIFF_WIDTH
    h = _rms_bf16(x_ref[...].reshape(m, d), g_ref[...], NORM_EPS)
    z = jnp.dot(h, w_ref[...], preferred_element_type=F32)
    vo_ref[...] = z[:, w:2 * w].reshape(bb, ts, w)
    ko_ref[...] = z[:, 2 * w:3 * w].reshape(bb, ts, w)

    @pl.when(s == 0)
    def _():
        ext_ref[:, 0:HIST_ROWS, :] = hist_ref[...]

    ext_ref[:, HIST_ROWS:HIST_ROWS + ts, :] = z[:, 3 * w:].reshape(bb, ts, POOL_WIDTH)
    pos = start + s * ts + lax.broadcasted_iota(jnp.int32, (ts, 1), 0)
    for gi, win_len in enumerate(POOL_WINDOWS):
        cols = slice(gi * POOL_GC, (gi + 1) * POOL_GC)
        inv_cnt = 1.0 / jnp.minimum(win_len, pos + 1).astype(F32)
        pooled = []
        for i in range(bb):
            u = ext_ref[i, HIST_ROWS:HIST_ROWS + ts, cols]
            win = u
            for k in range(1, win_len):
                win = win + ext_ref[i, HIST_ROWS - k:HIST_ROWS - k + ts, cols]
            pooled.append((win * inv_cnt - u).astype(BF16))
        pooled = pooled[0] if bb == 1 else jnp.concatenate(pooled, axis=0)
        y = jnp.dot(pooled, wp_ref[gi], preferred_element_type=F32) * ps_ref[:, cols]
        py_ref[:, :, cols] = y.astype(BF16).reshape(bb, ts, POOL_GC)
    tail = ext_ref[:, ts:ts + HIST_ROWS, :]
    ph_ref[...] = tail
    ext_ref[:, 0:HIST_ROWS, :] = tail
    return h, z[:, 0:w]


def _in_proj_prompt_kernel(x_ref, g_ref, w_ref, wt_ref, hist_ref, wp_ref, ps_ref,
                           qvt_ref, ka_ref, ko_ref, vo_ref, py_ref, ph_ref, ext_ref, *, ts, start):
    h, k_hm = _proj_and_pool(x_ref, g_ref, w_ref, hist_ref, wp_ref, ps_ref, ko_ref, vo_ref, py_ref, ph_ref,
                             ext_ref, 1, ts, start)
    ka_ref[...] = k_hm.astype(BF16)
    qvt_ref[...] = lax.dot_general(wt_ref[...], h, _NT, preferred_element_type=F32).astype(BF16)


def _in_proj_sample_kernel(x_ref, g_ref, w_ref, hist_ref, wp_ref, ps_ref,
                           q_ref, ko_ref, vo_ref, py_ref, ph_ref, ext_ref, *, bb, ts, start):
    _, q = _proj_and_pool(x_ref, g_ref, w_ref, hist_ref, wp_ref, ps_ref, ko_ref, vo_ref, py_ref, ph_ref,
                          ext_ref, bb, ts, start)
    q_ref[...] = q.astype(BF16).reshape(bb, ts, DIFF_WIDTH)


def _in_proj(x, g, w_cols, w_rows_t, hist, w_pool, pool_scale, start):
    batch, seq, d = x.shape
    bb, ts = _row_tiling(batch, seq)
    assert seq >= HIST_ROWS
    prompt = w_rows_t is not None
    w = DIFF_WIDTH
    tok = lambda width: pl.BlockSpec((bb, ts, width), lambda b, s: (b, s, 0))
    hist_spec = pl.BlockSpec((bb, HIST_ROWS, POOL_WIDTH), lambda b, s: (b, 0, 0))
    tok_shape = lambda width, dt: jax.ShapeDtypeStruct((batch, seq, width), dt)
    common_out_specs = [tok(w), tok(w), tok(POOL_WIDTH), hist_spec]
    common_out_shape = [tok_shape(w, F32), tok_shape(w, F32), tok_shape(POOL_WIDTH, BF16),
                        jax.ShapeDtypeStruct((batch, HIST_ROWS, POOL_WIDTH), F32)]
    tail_specs = [hist_spec, _const_spec(w_pool.shape), _const_spec(pool_scale.shape)]
    if prompt:
        assert bb == 1
        rows_t = w_rows_t.shape[0]
        body = functools.partial(_in_proj_prompt_kernel, ts=ts, start=start)
        in_specs = [tok(d), _const_spec(g.shape), _const_spec(w_cols.shape), _const_spec(w_rows_t.shape)] + tail_specs
        out_specs = [pl.BlockSpec((None, None, rows_t, ts), lambda b, s: (b, s, 0, 0)),
                     pl.BlockSpec((None, ts, w), lambda b, s: (b, s, 0))] + common_out_specs
        out_shape = [jax.ShapeDtypeStruct((batch, seq // ts, rows_t, ts), BF16), tok_shape(w, BF16)] + common_out_shape
        args = (x, g, w_cols, w_rows_t, hist, w_pool, pool_scale)
    else:
        body = functools.partial(_in_proj_sample_kernel, bb=bb, ts=ts, start=start)
        in_specs = [tok(d), _const_spec(g.shape), _const_spec(w_cols.shape)] + tail_specs
        out_specs = [tok(w)] + common_out_specs
        out_shape = [tok_shape(w, BF16)] + common_out_shape
        args = (x, g, w_cols, hist, w_pool, pool_scale)
    return pl.pallas_call(
        body,
        grid=(batch // bb, seq // ts),
        in_specs=in_specs,
        out_specs=out_specs,
        out_shape=out_shape,
        scratch_shapes=[pltpu.VMEM((bb, HIST_ROWS + ts, POOL_WIDTH), F32)],
        compiler_params=_params("arbitrary", "arbitrary"),
        name="in_proj_prompt" if prompt else "in_proj_sample",
    )(*args)


def _lambda(lq_ref, lk_ref, lam_init):
    e = jnp.exp(jnp.sum(lq_ref[...] * lk_ref[...], axis=-1, keepdims=True))
    return e[0:1] - e[1:2] + lam_init


def _prompt_attn_kernel(qt_ref, k_ref, vt_ref, lq_ref, lk_ref, sg_ref, o_ref, s_ref, acc_ref, *, t, lam_init):
    i = pl.program_id(2)
    qt = qt_ref[...]
    zero = jnp.zeros((DIFF_DK, t), BF16)
    q2t = jnp.concatenate([jnp.concatenate([qt[:DIFF_DK], zero], axis=0),
                           jnp.concatenate([zero, qt[DIFF_DK:]], axis=0)], axis=1)

    def scores(j, slot, mp):
        k_blk = k_ref[pl.ds(pl.multiple_of(j * t, t), t), :]
        s_ref[slot, mp] = jnp.dot(k_blk, q2t[:, mp * t:(mp + 1) * t], preferred_element_type=F32)

    def update(j, slot, mp, stats, masked):
        m, l = stats
        s = s_ref[slot, mp]
        if masked:
            key = lax.broadcasted_iota(jnp.int32, s.shape, 0)
            qry = lax.broadcasted_iota(jnp.int32, s.shape, 1)
            s = jnp.where(lax.shift_right_logical(key, CHUNK_SHIFT) <= lax.shift_right_logical(qry, CHUNK_SHIFT),
                          s, -jnp.inf)
        m_new = jnp.maximum(m, jnp.max(s, axis=0, keepdims=True))
        alpha = jnp.exp2(m - m_new)
        p = jnp.exp2(s - m_new)
        acc_ref[mp] = alpha * acc_ref[mp] + jnp.dot(vt_ref[j], p.astype(BF16), preferred_element_type=F32)
        return m_new, alpha * l + jnp.sum(p, axis=0, keepdims=True)

    def step(j, cur, nxt, st0, st1):
        scores(j + 1, nxt, 0)
        st0 = update(j, cur, 0, st0, False)
        scores(j + 1, nxt, 1)
        st1 = update(j, cur, 1, st1, False)
        return st0, st1

    def pair(p, carry):
        return step(2 * p + 1, 1, 0, *step(2 * p, 0, 1, *carry))

    def last(slot, st0, st1):
        return update(i, slot, 0, st0, True)[1], update(i, slot, 1, st1, True)[1]

    acc_ref[...] = jnp.zeros(acc_ref.shape, F32)
    init = (jnp.full((1, t), -jnp.inf, F32), jnp.zeros((1, t), F32))
    scores(0, 0, 0)
    scores(0, 0, 1)
    st0, st1 = lax.fori_loop(0, i // 2, pair, (init, init))
    l0, l1 = lax.cond(i % 2 == 1,
                      lambda a, b: last(1, *step(i - 1, 0, 1, a, b)),
                      lambda a, b: last(0, a, b),
                      st0, st1)
    o = acc_ref[0] * (1.0 / l0) - _lambda(lq_ref, lk_ref, lam_init) * (acc_ref[1] * (1.0 / l1))
    ms = jnp.mean(o * o, axis=0, keepdims=True)
    y = o * lax.rsqrt(ms + SUBLN_EPS) * sg_ref[...] * (1.0 - lam_init)
    o_ref[...] = y.T.astype(BF16)


def _prompt_attn(qvt, k, lam_q, lam_k, subln_g_col, lam_init):
    batch, n_blk, _, t = qvt.shape
    seq = n_blk * t
    assert t % CHUNK == 0 and (t & (t - 1)) == 0
    v_row0 = DIFF_WIDTH // DIFF_DV
    return pl.pallas_call(
        functools.partial(_prompt_attn_kernel, t=t, lam_init=lam_init),
        grid=(batch, N_DIFF_HEADS, n_blk),
        in_specs=[pl.BlockSpec((None, None, DIFF_DV, t), lambda b, h, i: (b, i, h, 0)),
                  pl.BlockSpec((None, seq, DIFF_DV), lambda b, h, i: (b, 0, h)),
                  pl.BlockSpec((None, n_blk, DIFF_DV, t), lambda b, h, i: (b, 0, v_row0 + h, 0)),
                  _const_spec(lam_q.shape), _const_spec(lam_k.shape), _const_spec(subln_g_col.shape)],
        out_specs=pl.BlockSpec((None, t, DIFF_DV), lambda b, h, i: (b, i, h)),
        out_shape=jax.ShapeDtypeStruct((batch, seq, DIFF_WIDTH), BF16),
        scratch_shapes=[pltpu.VMEM((2, 2, t, t), F32),
                        pltpu.VMEM((2, DIFF_DV, t), F32)],
        compiler_params=_params("arbitrary", "arbitrary", "arbitrary"),
        name="prompt_attn",
    )(qvt, k, qvt, lam_q, lam_k, subln_g_col)


def _sample_attn_kernel(q_ref, kn_ref, vn_ref, ck_ref, cv_ref, lq_ref, lk_ref, sg_ref, o_ref,
                        *, past, n_new, lam_init):
    lam = _lambda(lq_ref, lk_ref, lam_init)
    sg = sg_ref[...]
    lane = lax.broadcasted_iota(jnp.int32, (n_new, DIFF_DV), 1)
    r = lax.broadcasted_iota(jnp.int32, (n_new, n_new), 0)
    c = lax.broadcasted_iota(jnp.int32, (n_new, n_new), 1)
    visible = lax.shift_right_logical(past + c, CHUNK_SHIFT) <= lax.shift_right_logical(past + r, CHUNK_SHIFT)
    for h in range(N_DIFF_HEADS):
        hs = slice(h * DIFF_DV, (h + 1) * DIFF_DV)
        q_blk = q_ref[:, hs].astype(F32)
        v_c = cv_ref[:, hs].astype(BF16)
        v_n = vn_ref[:, hs].astype(BF16)
        half = h % 2
        o_maps = []
        for mp in range(2):
            qm = q_blk if mp == half else pltpu.roll(q_blk, DIFF_DK, axis=1)
            qm = jnp.where((lane >= DIFF_DK) == bool(half), qm, 0.0).astype(BF16)
            ks = slice((2 * mp + h // 2) * DIFF_DV, (2 * mp + h // 2 + 1) * DIFF_DV)
            s_c = lax.dot_general(qm, ck_ref[:, ks].astype(BF16), _NT, preferred_element_type=F32)
            s_n = lax.dot_general(qm, kn_ref[:, ks].astype(BF16), _NT, preferred_element_type=F32)
            s_n = jnp.where(visible, s_n, -jnp.inf)
            mx = jnp.maximum(jnp.max(s_c, axis=-1, keepdims=True), jnp.max(s_n, axis=-1, keepdims=True))
            p_c = jnp.exp2(s_c - mx)
            p_n = jnp.exp2(s_n - mx)
            l = jnp.sum(p_c, axis=-1, keepdims=True) + jnp.sum(p_n, axis=-1, keepdims=True)
            o = jnp.dot(p_c.astype(BF16), v_c, preferred_element_type=F32) + \
                jnp.dot(p_n.astype(BF16), v_n, preferred_element_type=F32)
            o_maps.append(o * (1.0 / l))
        o = o_maps[0] - lam * o_maps[1]
        ms = jnp.mean(o * o, axis=-1, keepdims=True)
        o_ref[:, hs] = (o * lax.rsqrt(ms + SUBLN_EPS) * sg * (1.0 - lam_init)).astype(BF16)


def _sample_attn(q, k_new, v_new, cache_k, cache_v, layer, lam_q, lam_k, subln_g, lam_init):
    batch, n_new, _ = q.shape
    past = cache_k.shape[2]
    w = DIFF_WIDTH
    new = pl.BlockSpec((None, n_new, w), lambda b: (b, 0, 0))
    cached = pl.BlockSpec((None, None, past, w), lambda b: (layer, b, 0, 0))
    return pl.pallas_call(
        functools.partial(_sample_attn_kernel, past=past, n_new=n_new, lam_init=lam_init),
        grid=(batch,),
        in_specs=[new, new, new, cached, cached,
                  _const_spec(lam_q.shape), _const_spec(lam_k.shape), _const_spec(subln_g.shape)],
        out_specs=new,
        out_shape=jax.ShapeDtypeStruct((batch, n_new, w), BF16),
        compiler_params=_params("arbitrary"),
        name="sample_attn",
    )(q, k_new, v_new, cache_k, cache_v, lam_q, lam_k, subln_g)


def _mem_kv_kernel(m_ref, g_ref, w_ref, k_ref, v_ref, kb_ref, vb_ref):
    d = m_ref.shape[-1]
    z = jnp.dot(_rms_bf16(m_ref[...], g_ref[...], NORM_EPS), w_ref[...], preferred_element_type=F32)
    k_ref[...] = z[:, :d]
    v_ref[...] = z[:, d:]
    kb_ref[...] = z[:, :d].astype(BF16)
    vb_ref[...] = z[:, d:].astype(BF16)


def _mem_kv(mem, g, w_kv):
    batch, n_mem, d = mem.shape
    rows = batch * n_mem
    tm = min(ROW_TILE, rows)
    assert rows % tm == 0
    tok = pl.BlockSpec((tm, d), lambda i: (i, 0))
    outs = pl.pallas_call(
        _mem_kv_kernel,
        grid=(rows // tm,),
        in_specs=[tok, _const_spec(g.shape), _const_spec(w_kv.shape)],
        out_specs=[tok] * 4,
        out_shape=[jax.ShapeDtypeStruct((rows, d), F32)] * 2 + [jax.ShapeDtypeStruct((rows, d), BF16)] * 2,
        compiler_params=_params("arbitrary"),
        name="mem_kv",
    )(mem.reshape(rows, d), g, w_kv)
    return [o.reshape(batch, n_mem, d) for o in outs]


def _mix_kernel(x_ref, a_ref, p_ref, wo_ref, g_ref, wq_ref, mk_ref, mv_ref, wox_ref, o_ref, *, bb, ts):
    d = x_ref.shape[-1]
    m = bb * ts
    hd = d // N_X_HEADS
    ap = jnp.concatenate([a_ref[...].reshape(m, -1), p_ref[...].reshape(m, -1)], axis=-1)
    x1 = x_ref[...].reshape(m, d) + jnp.dot(ap, wo_ref[...], preferred_element_type=F32)
    h = _rms_bf16(x1, g_ref[...], NORM_EPS)
    q = jnp.dot(h, wq_ref[...], preferred_element_type=F32).astype(BF16)
    rows = []
    for i in range(bb):
        heads = []
        for hh in range(N_X_HEADS):
            cs = slice(hh * hd, (hh + 1) * hd)
            s = lax.dot_general(q[i * ts:(i + 1) * ts, cs], mk_ref[i, :, cs], _NT, preferred_element_type=F32)
            p = jnp.exp(s - jnp.max(s, axis=-1, keepdims=True))
            inv_l = 1.0 / jnp.sum(p, axis=-1, keepdims=True)
            o = jnp.dot(p.astype(BF16), mv_ref[i, :, cs], preferred_element_type=F32) * inv_l
            heads.append(o.astype(BF16))
        rows.append(jnp.concatenate(heads, axis=-1))
    o_all = rows[0] if bb == 1 else jnp.concatenate(rows, axis=0)
    o_ref[...] = (x1 + jnp.dot(o_all, wox_ref[...], preferred_element_type=F32)).reshape(bb, ts, d)


def _mix(x, a, pool_y, w_out, g, wq, mk, mv, wo_x):
    batch, seq, d = x.shape
    bb, ts = _row_tiling(batch, seq)
    n_mem = mk.shape[1]
    tok = lambda width: pl.BlockSpec((bb, ts, width), lambda b, s: (b, s, 0))
    mem = pl.BlockSpec((bb, n_mem, d), lambda b, s: (b, 0, 0))
    return pl.pallas_call(
        functools.partial(_mix_kernel, bb=bb, ts=ts),
        grid=(batch // bb, seq // ts),
        in_specs=[tok(d), tok(a.shape[-1]), tok(pool_y.shape[-1]), _const_spec(w_out.shape), _const_spec(g.shape),
                  _const_spec(wq.shape), mem, mem, _const_spec(wo_x.shape)],
        out_specs=tok(d),
        out_shape=jax.ShapeDtypeStruct(x.shape, F32),
        compiler_params=_params("arbitrary", "arbitrary"),
        name="mix",
    )(x, a, pool_y, w_out, g, wq, mk, mv, wo_x)


def _mlp_kernel(x_ref, g_ref, wu_ref, wd_ref, fg_ref, o_ref, *, bb, ts, final):
    d = x_ref.shape[-1]
    m = bb * ts
    x = x_ref[...].reshape(m, d)
    h = _rms_bf16(x, g_ref[...], NORM_EPS)
    ck = wu_ref.shape[1] // MLP_CHUNKS
    acc = x
    for j in range(MLP_CHUNKS):
        up = jnp.maximum(jnp.dot(h, wu_ref[:, j * ck:(j + 1) * ck], preferred_element_type=F32), 0.0)
        acc = acc + jnp.dot((up * up).astype(BF16), wd_ref[j * ck:(j + 1) * ck, :], preferred_element_type=F32)
    if final:
        ms = jnp.mean(acc * acc, axis=-1, keepdims=True)
        acc = acc * lax.rsqrt(ms + NORM_EPS) * fg_ref[...]
    o_ref[...] = acc.reshape(bb, ts, d)


def _mlp(x, g, w_up, w_down, final_g, final):
    batch, seq, d = x.shape
    bb, ts = _row_tiling(batch, seq)
    assert w_up.shape[1] % MLP_CHUNKS == 0
    tok = pl.BlockSpec((bb, ts, d), lambda b, s: (b, s, 0))
    return pl.pallas_call(
        functools.partial(_mlp_kernel, bb=bb, ts=ts, final=final),
        grid=(batch // bb, seq // ts),
        in_specs=[tok, _const_spec(g.shape), _const_spec(w_up.shape), _const_spec(w_down.shape),
                  _const_spec(final_g.shape)],
        out_specs=tok,
        out_shape=jax.ShapeDtypeStruct(x.shape, F32),
        compiler_params=_params("arbitrary", "arbitrary"),
        name="mlp",
    )(x, g, w_up, w_down, final_g)


def _head_major(w):
    d = w.shape[0]
    return w.reshape(d, 2, N_DIFF_HEADS, DIFF_DK).transpose(0, 2, 1, 3).reshape(d, DIFF_WIDTH)


def _in_weights(w_in):
    w = DIFF_WIDTH
    wq, wk, wv, wu = w_in[:, :w], w_in[:, w:2 * w], w_in[:, 2 * w:3 * w], w_in[:, 3 * w:]
    wq = _head_major(wq) * (DIFF_DK ** -0.5 * LOG2E)
    cols_p = jnp.concatenate([_head_major(wk), wv, wk, wu], axis=1).astype(BF16)
    rows_t = jnp.concatenate([wq, wv], axis=1).T.astype(BF16)
    cols_s = jnp.concatenate([wq, wv, wk, wu], axis=1).astype(BF16)
    return cols_p, rows_t, cols_s


def kernel(x_prompt, x_sample, cache_k, cache_v, state_pool, cache_mem_k, cache_mem_v, mem_prompt,
           norm_mix_g, w_in, lam_q, lam_k, subln_g, w_pool, pool_scale, w_out,
           norm_x_g, norm_mem_g, wq_x, wk_x, wv_x, wo_x, norm_mlp_g, w_up, w_down, final_g):
    depth = w_in.shape[0]
    batch, seq, d = x_prompt.shape
    dec_batch, dec_seq, _ = x_sample.shape
    past = cache_k.shape[2]
    n_mem = mem_prompt.shape[1]
    hd = d // N_X_HEADS

    ck = cache_k.reshape(depth, dec_batch, past, DIFF_WIDTH)
    cv = cache_v.reshape(depth, dec_batch, past, DIFF_WIDTH)
    cmk = cache_mem_k.reshape(depth, dec_batch, n_mem, d).astype(BF16)
    cmv = cache_mem_v.reshape(depth, dec_batch, n_mem, d).astype(BF16)
    hist_s = jnp.pad(state_pool, ((0, 0), (0, 0), (HIST_ROWS - POOL_HIST, 0), (0, 0)))
    hist_p = jnp.zeros((batch, HIST_ROWS, POOL_WIDTH), F32)
    row = lambda v: v.reshape(1, -1)
    fg = row(final_g)

    xp, xs = x_prompt, x_sample
    outs = [[] for _ in range(8)]
    for l in range(depth):
        lam_init = 0.8 - 0.6 * math.exp(-0.3 * l)
        last = l == depth - 1
        w_cols_p, w_rows_t, w_cols_s = _in_weights(w_in[l])
        wp = w_pool[l].astype(BF16)
        ps = row(pool_scale[l])
        g_mix, g_x, g_mlp, sg = row(norm_mix_g[l]), row(norm_x_g[l]), row(norm_mlp_g[l]), row(subln_g[l])
        w_o = w_out[l].astype(BF16)
        wq = (wq_x[l] * hd ** -0.5).astype(BF16)
        wo = wo_x[l].astype(BF16)
        wu, wd = w_up[l].astype(BF16), w_down[l].astype(BF16)
        w_kv = jnp.concatenate([wk_x[l], wv_x[l]], axis=1).astype(BF16)

        qvt, ka, ko, vo, py, ph = _in_proj(xp, g_mix, w_cols_p, w_rows_t, hist_p, wp, ps, 0)
        a = _prompt_attn(qvt, ka, lam_q[l], lam_k[l], sg.reshape(-1, 1), lam_init)
        mk, mv, mkb, mvb = _mem_kv(mem_prompt, row(norm_mem_g[l]), w_kv)
        xp = _mix(xp, a, py, w_o, g_x, wq, mkb, mvb, wo)
        xp = _mlp(xp, g_mlp, wu, wd, fg, last)
        for dst, val in zip(outs[:5], (ko, vo, ph[:, HIST_ROWS - POOL_HIST:], mk, mv)):
            dst.append(val)

        q, ko, vo, py, ph = _in_proj(xs, g_mix, w_cols_s, None, hist_s[l], wp, ps, past)
        a = _sample_attn(q, ko, vo, ck, cv, l, lam_q[l], lam_k[l], sg, lam_init)
        xs = _mix(xs, a, py, w_o, g_x, wq, cmk[l], cmv[l], wo)
        xs = _mlp(xs, g_mlp, wu, wd, fg, last)
        for dst, val in zip(outs[5:], (ko, vo, ph[:, HIST_ROWS - POOL_HIST:])):
            dst.append(val)

    kp, vp, pp, mkp, mvp, ks, vs, pps = [jnp.stack(o) for o in outs]
    return (xp, xs,
            kp.reshape(depth, batch, seq, 2, N_DIFF_HEADS, DIFF_DK),
            vp.reshape(depth, batch, seq, N_DIFF_HEADS, DIFF_DV),
            pp,
            mkp.reshape(depth, batch, n_mem, N_X_HEADS, hd),
            mvp.reshape(depth, batch, n_mem, N_X_HEADS, hd),
            ks.reshape(depth, dec_batch, dec_seq, 2, N_DIFF_HEADS, DIFF_DK),
            vs.reshape(depth, dec_batch, dec_seq, N_DIFF_HEADS, DIFF_DV),
            pps)
```

```python
import functools
import math

import jax
import jax.numpy as jnp
from jax import lax
from jax.experimental import pallas as pl
from jax.experimental.pallas import tpu as pltpu

F32 = jnp.float32
BF16 = jnp.bfloat16

CHUNK = 64
CHUNK_SHIFT = CHUNK.bit_length() - 1
N_DIFF_HEADS = 4
DIFF_DK = 64
DIFF_DV = 2 * DIFF_DK
DIFF_WIDTH = N_DIFF_HEADS * DIFF_DV
V_AUG_ROWS = DIFF_DV + 16
POOL_WINDOWS = (2, 4, 8, 16)
POOL_GC = 128
POOL_WIDTH = len(POOL_WINDOWS) * POOL_GC
assert all(w & (w - 1) == 0 for w in POOL_WINDOWS)
POOL_HIST = max(POOL_WINDOWS) - 1
HIST_ROWS = 16
N_X_HEADS = 4
NORM_EPS = 1e-6
SUBLN_EPS = 1e-5
LOG2E = 1.4426950408889634

ROW_TILE = 512
MLP_CHUNKS = 4
V7X_VMEM_LIMIT = 56 * 2**20

_NT = (((1,), (1,)), ((), ()))


def _rms_bf16(x, g, eps):
    ms = jnp.mean(x * x, axis=-1, keepdims=True)
    return (x * lax.rsqrt(ms + eps) * g).astype(BF16)


def _row_tiling(batch, seq):
    if seq >= ROW_TILE:
        assert seq % ROW_TILE == 0
        return 1, ROW_TILE
    bb = max(1, min(batch, ROW_TILE // seq))
    assert batch % bb == 0 and seq % 8 == 0
    return bb, seq


def _params(*semantics):
    return pltpu.CompilerParams(dimension_semantics=semantics, vmem_limit_bytes=V7X_VMEM_LIMIT)


def _const_spec(shape):
    nd = len(shape)
    return pl.BlockSpec(shape, lambda *_: (0,) * nd, pipeline_mode=pl.Buffered(1))


def _project(x_ref, g_ref, w_ref, ko_ref, vo_ref, ext_ref, hist_ref, bb, ts):
    d = x_ref.shape[-1]
    m = bb * ts
    w = DIFF_WIDTH
    h = _rms_bf16(x_ref[...].reshape(m, d), g_ref[...], NORM_EPS)
    z = jnp.dot(h, w_ref[...], preferred_element_type=F32)
    vo_ref[...] = z[:, w:2 * w].reshape(bb, ts, w)
    ko_ref[...] = z[:, 2 * w:3 * w].reshape(bb, ts, w)

    @pl.when(pl.program_id(1) == 0)
    def _():
        ext_ref[:, 0:HIST_ROWS, :] = hist_ref[...]

    ext_ref[:, HIST_ROWS:HIST_ROWS + ts, :] = z[:, 3 * w:].reshape(bb, ts, POOL_WIDTH)
    return h, z[:, 0:w]


def _pool_windows(ext_ref, ph_ref, bb, ts, start):
    pos = start + pl.program_id(1) * ts + lax.broadcasted_iota(jnp.int32, (ts, 1), 0)
    groups = []
    for gi, win_len in enumerate(POOL_WINDOWS):
        cols = slice(gi * POOL_GC, (gi + 1) * POOL_GC)
        inv_cnt = 1.0 / jnp.minimum(win_len, pos + 1).astype(F32)
        pooled = []
        for i in range(bb):
            win = ext_ref[i, :, cols]
            k = 1
            while k < win_len:
                win = win + pltpu.roll(win, k, axis=0)
                k *= 2
            u = ext_ref[i, HIST_ROWS:HIST_ROWS + ts, cols]
            pooled.append((win[HIST_ROWS:] * inv_cnt - u).astype(BF16))
        groups.append(pooled[0] if bb == 1 else jnp.concatenate(pooled, axis=0))
    tail = ext_ref[:, ts:ts + HIST_ROWS, :]
    ph_ref[...] = tail
    ext_ref[:, 0:HIST_ROWS, :] = tail
    return groups


def _pool_project(groups, wp_ref, ps_ref, py_ref, bb, ts):
    for gi, pooled in enumerate(groups):
        cols = slice(gi * POOL_GC, (gi + 1) * POOL_GC)
        y = jnp.dot(pooled, wp_ref[gi], preferred_element_type=F32) * ps_ref[:, cols]
        py_ref[:, :, cols] = y.astype(BF16).reshape(bb, ts, POOL_GC)


def _in_proj_prompt_kernel(x_ref, g_ref, w_ref, wt_ref, hist_ref, wp_ref, ps_ref, *refs, ts, start):
    qt_ref, vta_ref, ka_ref, ko_ref, vo_ref, py_ref, ph_ref, ext_ref = refs[-8:]
    h, k_hm = _project(x_ref, g_ref, w_ref, ko_ref, vo_ref, ext_ref, hist_ref, 1, ts)
    ka_ref[...] = k_hm.astype(BF16)
    groups = _pool_windows(ext_ref, ph_ref, 1, ts, start)
    zt = lax.dot_general(wt_ref[...], h, _NT, preferred_element_type=F32).astype(BF16)
    w = DIFF_WIDTH
    qt_ref[...] = zt[:w]
    ones = (lax.broadcasted_iota(jnp.int32, (V_AUG_ROWS - DIFF_DV, ts), 0) == 0).astype(BF16)
    for hd in range(N_DIFF_HEADS):
        vta_ref[hd * V_AUG_ROWS:hd * V_AUG_ROWS + DIFF_DV, :] = zt[w + hd * DIFF_DV:w + (hd + 1) * DIFF_DV]
        vta_ref[hd * V_AUG_ROWS + DIFF_DV:(hd + 1) * V_AUG_ROWS, :] = ones
    _pool_project(groups, wp_ref, ps_ref, py_ref, 1, ts)


def _in_proj_sample_kernel(x_ref, g_ref, w_ref, hist_ref, wp_ref, ps_ref,
                           q_ref, ko_ref, vo_ref, py_ref, ph_ref, ext_ref, *, bb, ts, start):
    _, q = _project(x_ref, g_ref, w_ref, ko_ref, vo_ref, ext_ref, hist_ref, bb, ts)
    q_ref[...] = q.astype(BF16).reshape(bb, ts, DIFF_WIDTH)
    _pool_project(_pool_windows(ext_ref, ph_ref, bb, ts, start), wp_ref, ps_ref, py_ref, bb, ts)


def _in_proj_common(batch, seq, bb, ts):
    assert seq >= HIST_ROWS
    tok = lambda width: pl.BlockSpec((bb, ts, width), lambda b, s: (b, s, 0))
    hist_spec = pl.BlockSpec((bb, HIST_ROWS, POOL_WIDTH), lambda b, s: (b, 0, 0))
    tok_shape = lambda width, dt: jax.ShapeDtypeStruct((batch, seq, width), dt)
    pool_specs = [tok(POOL_WIDTH), hist_spec]
    pool_shapes = [tok_shape(POOL_WIDTH, BF16), jax.ShapeDtypeStruct((batch, HIST_ROWS, POOL_WIDTH), F32)]
    return tok, hist_spec, tok_shape, pool_specs, pool_shapes


def _in_proj_prompt(x, g, w_cols, w_rows_t, hist, w_pool, pool_scale, layer, depth, kv_all):
    batch, seq, d = x.shape
    bb, ts = _row_tiling(batch, seq)
    assert bb == 1
    w = DIFF_WIDTH
    tok, hist_spec, tok_shape, pool_specs, pool_shapes = _in_proj_common(batch, seq, bb, ts)
    va_rows = N_DIFF_HEADS * V_AUG_ROWS
    tile_t = lambda rows: pl.BlockSpec((None, None, rows, ts), lambda b, s: (b, s, 0, 0))
    tile_t_shape = lambda rows: jax.ShapeDtypeStruct((batch, seq // ts, rows, ts), BF16)
    layered = pl.BlockSpec((None, bb, ts, w), lambda b, s: (layer, b, s, 0))
    in_specs = [tok(d), _const_spec(g.shape), _const_spec(w_cols.shape), _const_spec(w_rows_t.shape),
                hist_spec, _const_spec(w_pool.shape), _const_spec(pool_scale.shape)]
    args = [x, g, w_cols, w_rows_t, hist, w_pool, pool_scale]
    aliases = {}
    if kv_all is not None:
        aliases = {len(args): 3, len(args) + 1: 4}
        in_specs += [pl.BlockSpec(memory_space=pl.ANY)] * 2
        args += list(kv_all)
    return pl.pallas_call(
        functools.partial(_in_proj_prompt_kernel, ts=ts, start=0),
        grid=(batch // bb, seq // ts),
        in_specs=in_specs,
        out_specs=[tile_t(w), tile_t(va_rows),
                   pl.BlockSpec((None, ts, w), lambda b, s: (b, s, 0)), layered, layered] + pool_specs,
        out_shape=[tile_t_shape(w), tile_t_shape(va_rows), tok_shape(w, BF16),
                   jax.ShapeDtypeStruct((depth, batch, seq, w), F32),
                   jax.ShapeDtypeStruct((depth, batch, seq, w), F32)] + pool_shapes,
        input_output_aliases=aliases,
        scratch_shapes=[pltpu.VMEM((bb, HIST_ROWS + ts, POOL_WIDTH), F32)],
        compiler_params=_params("arbitrary", "arbitrary"),
        name="in_proj_prompt",
    )(*args)


def _in_proj_sample(x, g, w_cols, hist, w_pool, pool_scale, start):
    batch, seq, d = x.shape
    bb, ts = _row_tiling(batch, seq)
    w = DIFF_WIDTH
    tok, hist_spec, tok_shape, pool_specs, pool_shapes = _in_proj_common(batch, seq, bb, ts)
    return pl.pallas_call(
        functools.partial(_in_proj_sample_kernel, bb=bb, ts=ts, start=start),
        grid=(batch // bb, seq // ts),
        in_specs=[tok(d), _const_spec(g.shape), _const_spec(w_cols.shape),
                  hist_spec, _const_spec(w_pool.shape), _const_spec(pool_scale.shape)],
        out_specs=[tok(w), tok(w), tok(w)] + pool_specs,
        out_shape=[tok_shape(w, BF16), tok_shape(w, F32), tok_shape(w, F32)] + pool_shapes,
        scratch_shapes=[pltpu.VMEM((bb, HIST_ROWS + ts, POOL_WIDTH), F32)],
        compiler_params=_params("arbitrary", "arbitrary"),
        name="in_proj_sample",
    )(x, g, w_cols, hist, w_pool, pool_scale)


def _lambda(lq_ref, lk_ref, lam_init):
    e = jnp.exp(jnp.sum(lq_ref[...] * lk_ref[...], axis=-1, keepdims=True))
    return e[0:1] - e[1:2] + lam_init


def _prompt_attn_kernel(qt_ref, k_ref, vt_ref, lq_ref, lk_ref, sg_ref, o_ref, s_ref, acc_ref, *, t, n_blk, lam_init):
    lam = _lambda(lq_ref, lk_ref, lam_init)

    def masked_q(i, mp):
        zero = jnp.zeros((DIFF_DK, t), BF16)
        if mp == 0:
            return jnp.concatenate([qt_ref[i, :DIFF_DK, :], zero], axis=0)
        return jnp.concatenate([zero, qt_ref[i, DIFF_DK:, :]], axis=0)

    def scores(i, j, slot, mp):
        k_blk = k_ref[pl.ds(pl.multiple_of(j * t, t), t), :]
        s_ref[slot, mp] = jnp.dot(k_blk, masked_q(i, mp), preferred_element_type=F32)

    def update(j, slot, mp, m, masked):
        s = s_ref[slot, mp]
        if masked:
            key = lax.broadcasted_iota(jnp.int32, s.shape, 0)
            qry = lax.broadcasted_iota(jnp.int32, s.shape, 1)
            s = jnp.where(lax.shift_right_logical(key, CHUNK_SHIFT) <= lax.shift_right_logical(qry, CHUNK_SHIFT),
                          s, -jnp.inf)
        m_new = jnp.maximum(m, jnp.max(s, axis=0, keepdims=True))
        alpha = jnp.exp2(m - m_new)
        p = jnp.exp2(s - m_new)
        acc_ref[mp] = alpha * acc_ref[mp] + jnp.dot(vt_ref[j], p.astype(BF16), preferred_element_type=F32)
        return m_new

    def step(i, j, cur, nxt, m0, m1):
        scores(i, j + 1, nxt, 0)
        m0 = update(j, cur, 0, m0, False)
        scores(i, j + 1, nxt, 1)
        m1 = update(j, cur, 1, m1, False)
        return m0, m1

    def finish(i, slot, m0, m1, next_first):
        if next_first is not None:
            scores(i + 1, 0, next_first, 0)
        update(i, slot, 0, m0, True)
        if next_first is not None:
            scores(i + 1, 0, next_first, 1)
        update(i, slot, 1, m1, True)
        normalised = lambda acc: acc[:DIFF_DV] * (1.0 / acc[DIFF_DV:DIFF_DV + 1])
        o = normalised(acc_ref[0]) - lam * normalised(acc_ref[1])
        ms = jnp.mean(o * o, axis=0, keepdims=True)
        y = o * lax.rsqrt(ms + SUBLN_EPS) * sg_ref[...] * (1.0 - lam_init)
        o_ref[pl.ds(pl.multiple_of(i * t, t), t), :] = y.T.astype(BF16)

    def start():
        acc_ref[...] = jnp.zeros(acc_ref.shape, F32)
        return jnp.full((1, t), -jnp.inf, F32)

    def query_block(i, odd, next_first):
        init = start()
        m0, m1 = step(i, 0, 3 if odd else 2, 0, init, init)
        pair = lambda p, c: step(i, 2 * p + 2, 1, 0, *step(i, 2 * p + 1, 0, 1, *c))
        m0, m1 = lax.fori_loop(0, (i - 1) // 2, pair, (m0, m1))
        if odd:
            finish(i, 0, m0, m1, next_first)
        else:
            finish(i, 1, *step(i, i - 1, 0, 1, m0, m1), next_first)

    scores(0, 0, 2, 0)
    scores(0, 0, 2, 1)
    init = start()
    finish(0, 2, init, init, 3 if n_blk > 1 else None)
    if n_blk > 1:
        def two_query_blocks(ip, carry):
            query_block(2 * ip + 1, True, 2)
            query_block(2 * ip + 2, False, 3)
            return carry

        lax.fori_loop(0, (n_blk - 2) // 2, two_query_blocks, 0)
        query_block(n_blk - 1, True, None)


def _prompt_attn(qt, k, vta, lam_q, lam_k, subln_g_col, lam_init):
    batch, n_blk, _, t = qt.shape
    seq = n_blk * t
    assert t % CHUNK == 0 and (t & (t - 1)) == 0
    assert n_blk == 1 or n_blk % 2 == 0
    return pl.pallas_call(
        functools.partial(_prompt_attn_kernel, t=t, n_blk=n_blk, lam_init=lam_init),
        grid=(batch, N_DIFF_HEADS),
        in_specs=[pl.BlockSpec((None, n_blk, DIFF_DV, t), lambda b, h: (b, 0, h, 0)),
                  pl.BlockSpec((None, seq, DIFF_DV), lambda b, h: (b, 0, h)),
                  pl.BlockSpec((None, n_blk, V_AUG_ROWS, t), lambda b, h: (b, 0, h, 0)),
                  _const_spec(lam_q.shape), _const_spec(lam_k.shape), _const_spec(subln_g_col.shape)],
        out_specs=pl.BlockSpec((None, seq, DIFF_DV), lambda b, h: (b, 0, h)),
        out_shape=jax.ShapeDtypeStruct((batch, seq, DIFF_WIDTH), BF16),
        scratch_shapes=[pltpu.VMEM((4, 2, t, t), F32),
                        pltpu.VMEM((2, V_AUG_ROWS, t), F32)],
        compiler_params=_params("arbitrary", "arbitrary"),
        name="prompt_attn",
    )(qt, k, vta, lam_q, lam_k, subln_g_col)


def _sample_attn_kernel(q_ref, kn_ref, vn_ref, ck_ref, cv_ref, lq_ref, lk_ref, sg_ref, o_ref,
                        *, past, n_new, lam_init):
    lam = _lambda(lq_ref, lk_ref, lam_init)
    sg = sg_ref[...]
    lane = lax.broadcasted_iota(jnp.int32, (n_new, DIFF_DV), 1)
    r = lax.broadcasted_iota(jnp.int32, (n_new, n_new), 0)
    c = lax.broadcasted_iota(jnp.int32, (n_new, n_new), 1)
    visible = lax.shift_right_logical(past + c, CHUNK_SHIFT) <= lax.shift_right_logical(past + r, CHUNK_SHIFT)
    for h in range(N_DIFF_HEADS):
        hs = slice(h * DIFF_DV, (h + 1) * DIFF_DV)
        q_blk = q_ref[:, hs].astype(F32)
        v_c = cv_ref[:, hs].astype(BF16)
        v_n = vn_ref[:, hs].astype(BF16)
        half = h % 2
        o_maps = []
        for mp in range(2):
            qm = q_blk if mp == half else pltpu.roll(q_blk, DIFF_DK, axis=1)
            qm = jnp.where((lane >= DIFF_DK) == bool(half), qm, 0.0).astype(BF16)
            ks = slice((2 * mp + h // 2) * DIFF_DV, (2 * mp + h // 2 + 1) * DIFF_DV)
            s_c = lax.dot_general(qm, ck_ref[:, ks].astype(BF16), _NT, preferred_element_type=F32)
            s_n = lax.dot_general(qm, kn_ref[:, ks].astype(BF16), _NT, preferred_element_type=F32)
            s_n = jnp.where(visible, s_n, -jnp.inf)
            mx = jnp.maximum(jnp.max(s_c, axis=-1, keepdims=True), jnp.max(s_n, axis=-1, keepdims=True))
            p_c = jnp.exp2(s_c - mx)
            p_n = jnp.exp2(s_n - mx)
            l = jnp.sum(p_c, axis=-1, keepdims=True) + jnp.sum(p_n, axis=-1, keepdims=True)
            o = jnp.dot(p_c.astype(BF16), v_c, preferred_element_type=F32) + \
                jnp.dot(p_n.astype(BF16), v_n, preferred_element_type=F32)
            o_maps.append(o * (1.0 / l))
        o = o_maps[0] - lam * o_maps[1]
        ms = jnp.mean(o * o, axis=-1, keepdims=True)
        o_ref[:, hs] = (o * lax.rsqrt(ms + SUBLN_EPS) * sg * (1.0 - lam_init)).astype(BF16)


def _sample_attn(q, k_new, v_new, cache_k, cache_v, layer, lam_q, lam_k, subln_g, lam_init):
    batch, n_new, _ = q.shape
    past = cache_k.shape[2]
    w = DIFF_WIDTH
    new = pl.BlockSpec((None, n_new, w), lambda b: (b, 0, 0))
    cached = pl.BlockSpec((None, None, past, w), lambda b: (layer, b, 0, 0))
    return pl.pallas_call(
        functools.partial(_sample_attn_kernel, past=past, n_new=n_new, lam_init=lam_init),
        grid=(batch,),
        in_specs=[new, new, new, cached, cached,
                  _const_spec(lam_q.shape), _const_spec(lam_k.shape), _const_spec(subln_g.shape)],
        out_specs=new,
        out_shape=jax.ShapeDtypeStruct((batch, n_new, w), BF16),
        compiler_params=_params("arbitrary"),
        name="sample_attn",
    )(q, k_new, v_new, cache_k, cache_v, lam_q, lam_k, subln_g)


def _mem_kv_kernel(m_ref, g_ref, w_ref, k_ref, v_ref, kb_ref, vb_ref):
    d = m_ref.shape[-1]
    z = jnp.dot(_rms_bf16(m_ref[...], g_ref[...], NORM_EPS), w_ref[...], preferred_element_type=F32)
    k_ref[...] = z[:, :d]
    v_ref[...] = z[:, d:]
    kb_ref[...] = z[:, :d].astype(BF16)
    vb_ref[...] = z[:, d:].astype(BF16)


def _mem_kv(mem, g, w_kv):
    batch, n_mem, d = mem.shape
    rows = batch * n_mem
    tm = min(ROW_TILE, rows)
    assert rows % tm == 0
    tok = pl.BlockSpec((tm, d), lambda i: (i, 0))
    outs = pl.pallas_call(
        _mem_kv_kernel,
        grid=(rows // tm,),
        in_specs=[tok, _const_spec(g.shape), _const_spec(w_kv.shape)],
        out_specs=[tok] * 4,
        out_shape=[jax.ShapeDtypeStruct((rows, d), F32)] * 2 + [jax.ShapeDtypeStruct((rows, d), BF16)] * 2,
        compiler_params=_params("arbitrary"),
        name="mem_kv",
    )(mem.reshape(rows, d), g, w_kv)
    return [o.reshape(batch, n_mem, d) for o in outs]


def _mix_kernel(x_ref, a_ref, p_ref, wo_ref, g_ref, wq_ref, mk_ref, mv_ref, wox_ref, o_ref, *, bb, ts):
    d = x_ref.shape[-1]
    m = bb * ts
    hd = d // N_X_HEADS
    ap = jnp.concatenate([a_ref[...].reshape(m, -1), p_ref[...].reshape(m, -1)], axis=-1)
    x1 = x_ref[...].reshape(m, d) + jnp.dot(ap, wo_ref[...], preferred_element_type=F32)
    h = _rms_bf16(x1, g_ref[...], NORM_EPS)
    q = jnp.dot(h, wq_ref[...], preferred_element_type=F32).astype(BF16)
    rows = []
    for i in range(bb):
        heads = []
        for hh in range(N_X_HEADS):
            cs = slice(hh * hd, (hh + 1) * hd)
            s = lax.dot_general(q[i * ts:(i + 1) * ts, cs], mk_ref[i, :, cs], _NT, preferred_element_type=F32)
            p = jnp.exp(s - jnp.max(s, axis=-1, keepdims=True))
            inv_l = 1.0 / jnp.sum(p, axis=-1, keepdims=True)
            o = jnp.dot(p.astype(BF16), mv_ref[i, :, cs], preferred_element_type=F32) * inv_l
            heads.append(o.astype(BF16))
        rows.append(jnp.concatenate(heads, axis=-1))
    o_all = rows[0] if bb == 1 else jnp.concatenate(rows, axis=0)
    o_ref[...] = (x1 + jnp.dot(o_all, wox_ref[...], preferred_element_type=F32)).reshape(bb, ts, d)


def _mix(x, a, pool_y, w_out, g, wq, mk, mv, wo_x):
    batch, seq, d = x.shape
    bb, ts = _row_tiling(batch, seq)
    n_mem = mk.shape[1]
    tok = lambda width: pl.BlockSpec((bb, ts, width), lambda b, s: (b, s, 0))
    mem = pl.BlockSpec((bb, n_mem, d), lambda b, s: (b, 0, 0))
    return pl.pallas_call(
        functools.partial(_mix_kernel, bb=bb, ts=ts),
        grid=(batch // bb, seq // ts),
        in_specs=[tok(d), tok(a.shape[-1]), tok(pool_y.shape[-1]), _const_spec(w_out.shape), _const_spec(g.shape),
                  _const_spec(wq.shape), mem, mem, _const_spec(wo_x.shape)],
        out_specs=tok(d),
        out_shape=jax.ShapeDtypeStruct(x.shape, F32),
        compiler_params=_params("arbitrary", "arbitrary"),
        name="mix",
    )(x, a, pool_y, w_out, g, wq, mk, mv, wo_x)


def _mlp_kernel(x_ref, g_ref, wu_ref, wd_ref, fg_ref, o_ref, *, bb, ts, final):
    d = x_ref.shape[-1]
    m = bb * ts
    x = x_ref[...].reshape(m, d)
    h = _rms_bf16(x, g_ref[...], NORM_EPS)
    ck = wu_ref.shape[1] // MLP_CHUNKS
    acc = x
    for j in range(MLP_CHUNKS):
        up = jnp.maximum(jnp.dot(h, wu_ref[:, j * ck:(j + 1) * ck], preferred_element_type=F32), 0.0)
        acc = acc + jnp.dot((up * up).astype(BF16), wd_ref[j * ck:(j + 1) * ck, :], preferred_element_type=F32)
    if final:
        ms = jnp.mean(acc * acc, axis=-1, keepdims=True)
        acc = acc * lax.rsqrt(ms + NORM_EPS) * fg_ref[...]
    o_ref[...] = acc.reshape(bb, ts, d)


def _mlp(x, g, w_up, w_down, final_g, final):
    batch, seq, d = x.shape
    bb, ts = _row_tiling(batch, seq)
    assert w_up.shape[1] % MLP_CHUNKS == 0
    tok = pl.BlockSpec((bb, ts, d), lambda b, s: (b, s, 0))
    return pl.pallas_call(
        functools.partial(_mlp_kernel, bb=bb, ts=ts, final=final),
        grid=(batch // bb, seq // ts),
        in_specs=[tok, _const_spec(g.shape), _const_spec(w_up.shape), _const_spec(w_down.shape),
                  _const_spec(final_g.shape)],
        out_specs=tok,
        out_shape=jax.ShapeDtypeStruct(x.shape, F32),
        compiler_params=_params("arbitrary", "arbitrary"),
        name="mlp",
    )(x, g, w_up, w_down, final_g)


def _head_major(w):
    d = w.shape[0]
    return w.reshape(d, 2, N_DIFF_HEADS, DIFF_DK).transpose(0, 2, 1, 3).reshape(d, DIFF_WIDTH)


def _in_weights(w_in):
    w = DIFF_WIDTH
    wq, wk, wv, wu = w_in[:, :w], w_in[:, w:2 * w], w_in[:, 2 * w:3 * w], w_in[:, 3 * w:]
    wq = _head_major(wq) * (DIFF_DK ** -0.5 * LOG2E)
    cols_p = jnp.concatenate([_head_major(wk), wv, wk, wu], axis=1).astype(BF16)
    rows_t = jnp.concatenate([wq, wv], axis=1).T.astype(BF16)
    cols_s = jnp.concatenate([wq, wv, wk, wu], axis=1).astype(BF16)
    return cols_p, rows_t, cols_s


def kernel(x_prompt, x_sample, cache_k, cache_v, state_pool, cache_mem_k, cache_mem_v, mem_prompt,
           norm_mix_g, w_in, lam_q, lam_k, subln_g, w_pool, pool_scale, w_out,
           norm_x_g, norm_mem_g, wq_x, wk_x, wv_x, wo_x, norm_mlp_g, w_up, w_down, final_g):
    depth = w_in.shape[0]
    batch, seq, d = x_prompt.shape
    dec_batch, dec_seq, _ = x_sample.shape
    past = cache_k.shape[2]
    n_mem = mem_prompt.shape[1]
    hd = d // N_X_HEADS

    ck = cache_k.reshape(depth, dec_batch, past, DIFF_WIDTH)
    cv = cache_v.reshape(depth, dec_batch, past, DIFF_WIDTH)
    cmk = cache_mem_k.reshape(depth, dec_batch, n_mem, d).astype(BF16)
    cmv = cache_mem_v.reshape(depth, dec_batch, n_mem, d).astype(BF16)
    hist_s = jnp.pad(state_pool, ((0, 0), (0, 0), (HIST_ROWS - POOL_HIST, 0), (0, 0)))
    hist_p = jnp.zeros((batch, HIST_ROWS, POOL_WIDTH), F32)
    row = lambda v: v.reshape(1, -1)
    fg = row(final_g)

    xp, xs = x_prompt, x_sample
    outs = [[] for _ in range(6)]
    kv_prompt = None
    for l in range(depth):
        lam_init = 0.8 - 0.6 * math.exp(-0.3 * l)
        last = l == depth - 1
        w_cols_p, w_rows_t, w_cols_s = _in_weights(w_in[l])
        wp = w_pool[l].astype(BF16)
        ps = row(pool_scale[l])
        g_mix, g_x, g_mlp, sg = row(norm_mix_g[l]), row(norm_x_g[l]), row(norm_mlp_g[l]), row(subln_g[l])
        w_o = w_out[l].astype(BF16)
        wq = (wq_x[l] * hd ** -0.5).astype(BF16)
        wo = wo_x[l].astype(BF16)
        wu, wd = w_up[l].astype(BF16), w_down[l].astype(BF16)
        w_kv = jnp.concatenate([wk_x[l], wv_x[l]], axis=1).astype(BF16)

        qt, vta, ka, *kv_prompt, py, ph = _in_proj_prompt(xp, g_mix, w_cols_p, w_rows_t, hist_p, wp, ps, l, depth,
                                                          kv_prompt)
        a = _prompt_attn(qt, ka, vta, lam_q[l], lam_k[l], sg.reshape(-1, 1), lam_init)
        mk, mv, mkb, mvb = _mem_kv(mem_prompt, row(norm_mem_g[l]), w_kv)
        xp = _mix(xp, a, py, w_o, g_x, wq, mkb, mvb, wo)
        xp = _mlp(xp, g_mlp, wu, wd, fg, last)
        for dst, val in zip(outs[:3], (ph[:, HIST_ROWS - POOL_HIST:], mk, mv)):
            dst.append(val)

        q, ko, vo, py, ph = _in_proj_sample(xs, g_mix, w_cols_s, hist_s[l], wp, ps, past)
        a = _sample_attn(q, ko, vo, ck, cv, l, lam_q[l], lam_k[l], sg, lam_init)
        xs = _mix(xs, a, py, w_o, g_x, wq, cmk[l], cmv[l], wo)
        xs = _mlp(xs, g_mlp, wu, wd, fg, last)
        for dst, val in zip(outs[3:], (ko, vo, ph[:, HIST_ROWS - POOL_HIST:])):
            dst.append(val)

    kp, vp = kv_prompt
    pp, mkp, mvp, ks, vs, pps = [jnp.stack(o) for o in outs]
    return (xp, xs,
            kp.reshape(depth, batch, seq, 2, N_DIFF_HEADS, DIFF_DK),
            vp.reshape(depth, batch, seq, N_DIFF_HEADS, DIFF_DV),
            pp,
            mkp.reshape(depth, batch, n_mem, N_X_HEADS, hd),
            mvp.reshape(depth, batch, n_mem, N_X_HEADS, hd),
            ks.reshape(depth, dec_batch, dec_seq, 2, N_DIFF_HEADS, DIFF_DK),
            vs.reshape(depth, dec_batch, dec_seq, N_DIFF_HEADS, DIFF_DV),
            pps)
```

```python
import functools
import math

import jax
import jax.numpy as jnp
from jax import lax
from jax.experimental import pallas as pl
from jax.experimental.pallas import tpu as pltpu

F32 = jnp.float32
BF16 = jnp.bfloat16

CHUNK = 64
CHUNK_SHIFT = CHUNK.bit_length() - 1
N_DIFF_HEADS = 4
DIFF_DK = 64
DIFF_DV = 2 * DIFF_DK
DIFF_WIDTH = N_DIFF_HEADS * DIFF_DV
V_AUG_ROWS = DIFF_DV + 16
POOL_WINDOWS = (2, 4, 8, 16)
POOL_GC = 128
POOL_WIDTH = len(POOL_WINDOWS) * POOL_GC
assert all(w & (w - 1) == 0 for w in POOL_WINDOWS)
POOL_HIST = max(POOL_WINDOWS) - 1
HIST_ROWS = 16
N_X_HEADS = 4
NORM_EPS = 1e-6
SUBLN_EPS = 1e-5
LOG2E = 1.4426950408889634

ROW_TILE = 512
WIDE_ROW_TILE = 1024
MLP_CHUNKS = 4
V7X_VMEM_LIMIT = 56 * 2**20

_NT = (((1,), (1,)), ((), ()))


def _rms_bf16(x, g, eps):
    ms = jnp.mean(x * x, axis=-1, keepdims=True)
    return (x * lax.rsqrt(ms + eps) * g).astype(BF16)


def _row_tiling(batch, seq, rows=None):
    rows = ROW_TILE if rows is None else rows
    if seq >= rows:
        assert seq % rows == 0
        return 1, rows
    bb = max(1, min(batch, rows // seq))
    assert batch % bb == 0 and seq % 8 == 0
    return bb, seq


def _params(*semantics):
    return pltpu.CompilerParams(dimension_semantics=semantics, vmem_limit_bytes=V7X_VMEM_LIMIT)


def _const_spec(shape):
    nd = len(shape)
    return pl.BlockSpec(shape, lambda *_: (0,) * nd, pipeline_mode=pl.Buffered(1))


def _project(x_ref, g_ref, w_ref, ko_ref, vo_ref, ext_ref, hist_ref, ph_ref, bb, ts, start):
    d = x_ref.shape[-1]
    m = bb * ts
    w = DIFF_WIDTH
    h = _rms_bf16(x_ref[...].reshape(m, d), g_ref[...], NORM_EPS)
    u = jnp.dot(h, w_ref[:, 3 * w:], preferred_element_type=F32)

    @pl.when(pl.program_id(1) == 0)
    def _():
        ext_ref[:, 0:HIST_ROWS, :] = hist_ref[...]

    ext_ref[:, HIST_ROWS:HIST_ROWS + ts, :] = u.reshape(bb, ts, POOL_WIDTH)
    groups = _pool_windows(ext_ref, ph_ref, bb, ts, start)
    z = jnp.dot(h, w_ref[:, :3 * w], preferred_element_type=F32)
    vo_ref[...] = z[:, w:2 * w].reshape(bb, ts, w)
    ko_ref[...] = z[:, 2 * w:3 * w].reshape(bb, ts, w)
    return h, z[:, 0:w], groups


def _pool_windows(ext_ref, ph_ref, bb, ts, start):
    pos = start + pl.program_id(1) * ts + lax.broadcasted_iota(jnp.int32, (ts, 1), 0)
    groups = []
    for gi, win_len in enumerate(POOL_WINDOWS):
        cols = slice(gi * POOL_GC, (gi + 1) * POOL_GC)
        inv_cnt = 1.0 / jnp.minimum(win_len, pos + 1).astype(F32)
        pooled = []
        for i in range(bb):
            win = ext_ref[i, :, cols]
            k = 1
            while k < win_len:
                win = win + pltpu.roll(win, k, axis=0)
                k *= 2
            u = ext_ref[i, HIST_ROWS:HIST_ROWS + ts, cols]
            pooled.append((win[HIST_ROWS:] * inv_cnt - u).astype(BF16))
        groups.append(pooled[0] if bb == 1 else jnp.concatenate(pooled, axis=0))
    tail = ext_ref[:, ts:ts + HIST_ROWS, :]
    ph_ref[...] = tail
    ext_ref[:, 0:HIST_ROWS, :] = tail
    return groups


def _pool_project(groups, wp_ref, ps_ref, py_ref, bb, ts):
    for gi, pooled in enumerate(groups):
        cols = slice(gi * POOL_GC, (gi + 1) * POOL_GC)
        y = jnp.dot(pooled, wp_ref[gi], preferred_element_type=F32) * ps_ref[:, cols]
        py_ref[:, :, cols] = y.astype(BF16).reshape(bb, ts, POOL_GC)


def _in_proj_prompt_kernel(x_ref, g_ref, w_ref, wt_ref, hist_ref, wp_ref, ps_ref, *refs, ts, start):
    qt_ref, vta_ref, ka_ref, ko_ref, vo_ref, py_ref, ph_ref, ext_ref = refs[-8:]
    h, k_hm, groups = _project(x_ref, g_ref, w_ref, ko_ref, vo_ref, ext_ref, hist_ref, ph_ref, 1, ts, start)
    ka_ref[...] = k_hm.astype(BF16)
    _pool_project(groups, wp_ref, ps_ref, py_ref, 1, ts)
    zt = lax.dot_general(wt_ref[...], h, _NT, preferred_element_type=F32).astype(BF16)
    w = DIFF_WIDTH
    qt_ref[...] = zt[:w]
    ones = (lax.broadcasted_iota(jnp.int32, (V_AUG_ROWS - DIFF_DV, ts), 0) == 0).astype(BF16)
    for hd in range(N_DIFF_HEADS):
        vta_ref[hd * V_AUG_ROWS:hd * V_AUG_ROWS + DIFF_DV, :] = zt[w + hd * DIFF_DV:w + (hd + 1) * DIFF_DV]
        vta_ref[hd * V_AUG_ROWS + DIFF_DV:(hd + 1) * V_AUG_ROWS, :] = ones


def _in_proj_sample_kernel(x_ref, g_ref, w_ref, hist_ref, wp_ref, ps_ref,
                           q_ref, ko_ref, vo_ref, py_ref, ph_ref, ext_ref, *, bb, ts, start):
    _, q, groups = _project(x_ref, g_ref, w_ref, ko_ref, vo_ref, ext_ref, hist_ref, ph_ref, bb, ts, start)
    q_ref[...] = q.astype(BF16).reshape(bb, ts, DIFF_WIDTH)
    _pool_project(groups, wp_ref, ps_ref, py_ref, bb, ts)


def _in_proj_common(batch, seq, bb, ts):
    assert seq >= HIST_ROWS
    tok = lambda width: pl.BlockSpec((bb, ts, width), lambda b, s: (b, s, 0))
    hist_spec = pl.BlockSpec((bb, HIST_ROWS, POOL_WIDTH), lambda b, s: (b, 0, 0))
    tok_shape = lambda width, dt: jax.ShapeDtypeStruct((batch, seq, width), dt)
    pool_specs = [tok(POOL_WIDTH), hist_spec]
    pool_shapes = [tok_shape(POOL_WIDTH, BF16), jax.ShapeDtypeStruct((batch, HIST_ROWS, POOL_WIDTH), F32)]
    return tok, hist_spec, tok_shape, pool_specs, pool_shapes


def _in_proj_prompt(x, g, w_cols, w_rows_t, hist, w_pool, pool_scale, layer, depth, kv_all):
    batch, seq, d = x.shape
    bb, ts = _row_tiling(batch, seq)
    assert bb == 1
    w = DIFF_WIDTH
    tok, hist_spec, tok_shape, pool_specs, pool_shapes = _in_proj_common(batch, seq, bb, ts)
    va_rows = N_DIFF_HEADS * V_AUG_ROWS
    tile_t = lambda rows: pl.BlockSpec((None, None, rows, ts), lambda b, s: (b, s, 0, 0))
    tile_t_shape = lambda rows: jax.ShapeDtypeStruct((batch, seq // ts, rows, ts), BF16)
    layered = pl.BlockSpec((None, bb, ts, w), lambda b, s: (layer, b, s, 0))
    in_specs = [tok(d), _const_spec(g.shape), _const_spec(w_cols.shape), _const_spec(w_rows_t.shape),
                hist_spec, _const_spec(w_pool.shape), _const_spec(pool_scale.shape)]
    args = [x, g, w_cols, w_rows_t, hist, w_pool, pool_scale]
    aliases = {}
    if kv_all is not None:
        aliases = {len(args): 3, len(args) + 1: 4}
        in_specs += [pl.BlockSpec(memory_space=pl.ANY)] * 2
        args += list(kv_all)
    return pl.pallas_call(
        functools.partial(_in_proj_prompt_kernel, ts=ts, start=0),
        grid=(batch // bb, seq // ts),
        in_specs=in_specs,
        out_specs=[tile_t(w), tile_t(va_rows),
                   pl.BlockSpec((None, ts, w), lambda b, s: (b, s, 0)), layered, layered] + pool_specs,
        out_shape=[tile_t_shape(w), tile_t_shape(va_rows), tok_shape(w, BF16),
                   jax.ShapeDtypeStruct((depth, batch, seq, w), F32),
                   jax.ShapeDtypeStruct((depth, batch, seq, w), F32)] + pool_shapes,
        input_output_aliases=aliases,
        scratch_shapes=[pltpu.VMEM((bb, HIST_ROWS + ts, POOL_WIDTH), F32)],
        compiler_params=_params("arbitrary", "arbitrary"),
        name="in_proj_prompt",
    )(*args)


def _in_proj_sample(x, g, w_cols, hist, w_pool, pool_scale, start):
    batch, seq, d = x.shape
    bb, ts = _row_tiling(batch, seq)
    w = DIFF_WIDTH
    tok, hist_spec, tok_shape, pool_specs, pool_shapes = _in_proj_common(batch, seq, bb, ts)
    return pl.pallas_call(
        functools.partial(_in_proj_sample_kernel, bb=bb, ts=ts, start=start),
        grid=(batch // bb, seq // ts),
        in_specs=[tok(d), _const_spec(g.shape), _const_spec(w_cols.shape),
                  hist_spec, _const_spec(w_pool.shape), _const_spec(pool_scale.shape)],
        out_specs=[tok(w), tok(w), tok(w)] + pool_specs,
        out_shape=[tok_shape(w, BF16), tok_shape(w, F32), tok_shape(w, F32)] + pool_shapes,
        scratch_shapes=[pltpu.VMEM((bb, HIST_ROWS + ts, POOL_WIDTH), F32)],
        compiler_params=_params("arbitrary", "arbitrary"),
        name="in_proj_sample",
    )(x, g, w_cols, hist, w_pool, pool_scale)


def _lambda(lq_ref, lk_ref, lam_init):
    e = jnp.exp(jnp.sum(lq_ref[...] * lk_ref[...], axis=-1, keepdims=True))
    return e[0:1] - e[1:2] + lam_init


def _prompt_attn_kernel(qt_ref, k_ref, vt_ref, lq_ref, lk_ref, sg_ref, o_ref, s_ref, acc_ref, *, t, n_blk, lam_init):
    lam = _lambda(lq_ref, lk_ref, lam_init)

    def masked_q(i, mp):
        zero = jnp.zeros((DIFF_DK, t), BF16)
        if mp == 0:
            return jnp.concatenate([qt_ref[i, :DIFF_DK, :], zero], axis=0)
        return jnp.concatenate([zero, qt_ref[i, DIFF_DK:, :]], axis=0)

    def scores(i, j, slot, mp):
        k_blk = k_ref[pl.ds(pl.multiple_of(j * t, t), t), :]
        s_ref[slot, mp] = jnp.dot(k_blk, masked_q(i, mp), preferred_element_type=F32)

    def update(j, slot, mp, m, masked):
        s = s_ref[slot, mp]
        if masked:
            key = lax.broadcasted_iota(jnp.int32, s.shape, 0)
            qry = lax.broadcasted_iota(jnp.int32, s.shape, 1)
            s = jnp.where(lax.shift_right_logical(key, CHUNK_SHIFT) <= lax.shift_right_logical(qry, CHUNK_SHIFT),
                          s, -jnp.inf)
        m_new = jnp.maximum(m, jnp.max(s, axis=0, keepdims=True))
        alpha = jnp.exp2(m - m_new)
        p = jnp.exp2(s - m_new)
        acc_ref[mp] = alpha * acc_ref[mp] + jnp.dot(vt_ref[j], p.astype(BF16), preferred_element_type=F32)
        return m_new

    def step(i, j, cur, nxt, m0, m1):
        scores(i, j + 1, nxt, 0)
        m0 = update(j, cur, 0, m0, False)
        scores(i, j + 1, nxt, 1)
        m1 = update(j, cur, 1, m1, False)
        return m0, m1

    def finish(i, slot, m0, m1, next_first):
        if next_first is not None:
            scores(i + 1, 0, next_first, 0)
        update(i, slot, 0, m0, True)
        if next_first is not None:
            scores(i + 1, 0, next_first, 1)
        update(i, slot, 1, m1, True)
        normalised = lambda acc: acc[:DIFF_DV] * (1.0 / acc[DIFF_DV:DIFF_DV + 1])
        o = normalised(acc_ref[0]) - lam * normalised(acc_ref[1])
        ms = jnp.mean(o * o, axis=0, keepdims=True)
        y = o * lax.rsqrt(ms + SUBLN_EPS) * sg_ref[...] * (1.0 - lam_init)
        o_ref[pl.ds(pl.multiple_of(i * t, t), t), :] = y.T.astype(BF16)

    def start():
        acc_ref[...] = jnp.zeros(acc_ref.shape, F32)
        return jnp.full((1, t), -jnp.inf, F32)

    def query_block(i, odd, next_first):
        init = start()
        m0, m1 = step(i, 0, 3 if odd else 2, 0, init, init)
        pair = lambda j, c: step(i, j + 1, 1, 0, *step(i, j, 0, 1, *c))
        n_quads = (i - 1) // 4
        m0, m1 = lax.fori_loop(0, n_quads, lambda p, c: pair(4 * p + 3, pair(4 * p + 1, c)), (m0, m1))
        m0, m1 = lax.cond((i - 1) % 4 >= 2, lambda c: pair(4 * n_quads + 1, c), lambda c: c, (m0, m1))
        if odd:
            finish(i, 0, m0, m1, next_first)
        else:
            finish(i, 1, *step(i, i - 1, 0, 1, m0, m1), next_first)

    scores(0, 0, 2, 0)
    scores(0, 0, 2, 1)
    init = start()
    finish(0, 2, init, init, 3 if n_blk > 1 else None)
    if n_blk > 1:
        def two_query_blocks(ip, carry):
            query_block(2 * ip + 1, True, 2)
            query_block(2 * ip + 2, False, 3)
            return carry

        lax.fori_loop(0, (n_blk - 2) // 2, two_query_blocks, 0)
        query_block(n_blk - 1, True, None)


def _prompt_attn(qt, k, vta, lam_q, lam_k, subln_g_col, lam_init):
    batch, n_blk, _, t = qt.shape
    seq = n_blk * t
    assert t % CHUNK == 0 and (t & (t - 1)) == 0
    assert n_blk == 1 or n_blk % 2 == 0
    return pl.pallas_call(
        functools.partial(_prompt_attn_kernel, t=t, n_blk=n_blk, lam_init=lam_init),
        grid=(batch, N_DIFF_HEADS),
        in_specs=[pl.BlockSpec((None, n_blk, DIFF_DV, t), lambda b, h: (b, 0, h, 0)),
                  pl.BlockSpec((None, seq, DIFF_DV), lambda b, h: (b, 0, h)),
                  pl.BlockSpec((None, n_blk, V_AUG_ROWS, t), lambda b, h: (b, 0, h, 0)),
                  _const_spec(lam_q.shape), _const_spec(lam_k.shape), _const_spec(subln_g_col.shape)],
        out_specs=pl.BlockSpec((None, seq, DIFF_DV), lambda b, h: (b, 0, h)),
        out_shape=jax.ShapeDtypeStruct((batch, seq, DIFF_WIDTH), BF16),
        scratch_shapes=[pltpu.VMEM((4, 2, t, t), F32),
                        pltpu.VMEM((2, V_AUG_ROWS, t), F32)],
        compiler_params=_params("arbitrary", "arbitrary"),
        name="prompt_attn",
    )(qt, k, vta, lam_q, lam_k, subln_g_col)


def _sample_attn_kernel(q_ref, kn_ref, vn_ref, ck_ref, cv_ref, lq_ref, lk_ref, sg_ref, o_ref,
                        *, past, n_new, lam_init):
    lam = _lambda(lq_ref, lk_ref, lam_init)
    sg = sg_ref[...]
    lane = lax.broadcasted_iota(jnp.int32, (n_new, DIFF_DV), 1)
    r = lax.broadcasted_iota(jnp.int32, (n_new, n_new), 0)
    c = lax.broadcasted_iota(jnp.int32, (n_new, n_new), 1)
    visible = lax.shift_right_logical(past + c, CHUNK_SHIFT) <= lax.shift_right_logical(past + r, CHUNK_SHIFT)
    for h in range(N_DIFF_HEADS):
        hs = slice(h * DIFF_DV, (h + 1) * DIFF_DV)
        q_blk = q_ref[:, hs].astype(F32)
        v_c = cv_ref[:, hs].astype(BF16)
        v_n = vn_ref[:, hs].astype(BF16)
        half = h % 2
        o_maps = []
        for mp in range(2):
            qm = q_blk if mp == half else pltpu.roll(q_blk, DIFF_DK, axis=1)
            qm = jnp.where((lane >= DIFF_DK) == bool(half), qm, 0.0).astype(BF16)
            ks = slice((2 * mp + h // 2) * DIFF_DV, (2 * mp + h // 2 + 1) * DIFF_DV)
            s_c = lax.dot_general(qm, ck_ref[:, ks].astype(BF16), _NT, preferred_element_type=F32)
            s_n = lax.dot_general(qm, kn_ref[:, ks].astype(BF16), _NT, preferred_element_type=F32)
            s_n = jnp.where(visible, s_n, -jnp.inf)
            mx = jnp.maximum(jnp.max(s_c, axis=-1, keepdims=True), jnp.max(s_n, axis=-1, keepdims=True))
            p_c = jnp.exp2(s_c - mx)
            p_n = jnp.exp2(s_n - mx)
            l = jnp.sum(p_c, axis=-1, keepdims=True) + jnp.sum(p_n, axis=-1, keepdims=True)
            o = jnp.dot(p_c.astype(BF16), v_c, preferred_element_type=F32) + \
                jnp.dot(p_n.astype(BF16), v_n, preferred_element_type=F32)
            o_maps.append(o * (1.0 / l))
        o = o_maps[0] - lam * o_maps[1]
        ms = jnp.mean(o * o, axis=-1, keepdims=True)
        o_ref[:, hs] = (o * lax.rsqrt(ms + SUBLN_EPS) * sg * (1.0 - lam_init)).astype(BF16)


def _sample_attn(q, k_new, v_new, cache_k, cache_v, layer, lam_q, lam_k, subln_g, lam_init):
    batch, n_new, _ = q.shape
    past = cache_k.shape[2]
    w = DIFF_WIDTH
    new = pl.BlockSpec((None, n_new, w), lambda b: (b, 0, 0))
    cached = pl.BlockSpec((None, None, past, w), lambda b: (layer, b, 0, 0))
    return pl.pallas_call(
        functools.partial(_sample_attn_kernel, past=past, n_new=n_new, lam_init=lam_init),
        grid=(batch,),
        in_specs=[new, new, new, cached, cached,
                  _const_spec(lam_q.shape), _const_spec(lam_k.shape), _const_spec(subln_g.shape)],
        out_specs=new,
        out_shape=jax.ShapeDtypeStruct((batch, n_new, w), BF16),
        compiler_params=_params("arbitrary"),
        name="sample_attn",
    )(q, k_new, v_new, cache_k, cache_v, lam_q, lam_k, subln_g)


def _mem_kv_kernel(m_ref, g_ref, w_ref, k_ref, v_ref, kb_ref, vb_ref):
    d = m_ref.shape[-1]
    z = jnp.dot(_rms_bf16(m_ref[...], g_ref[...], NORM_EPS), w_ref[...], preferred_element_type=F32)
    k_ref[...] = z[:, :d]
    v_ref[...] = z[:, d:]
    kb_ref[...] = z[:, :d].astype(BF16)
    vb_ref[...] = z[:, d:].astype(BF16)


def _mem_kv(mem, g, w_kv):
    batch, n_mem, d = mem.shape
    rows = batch * n_mem
    tm = min(ROW_TILE, rows)
    assert rows % tm == 0
    tok = pl.BlockSpec((tm, d), lambda i: (i, 0))
    outs = pl.pallas_call(
        _mem_kv_kernel,
        grid=(rows // tm,),
        in_specs=[tok, _const_spec(g.shape), _const_spec(w_kv.shape)],
        out_specs=[tok] * 4,
        out_shape=[jax.ShapeDtypeStruct((rows, d), F32)] * 2 + [jax.ShapeDtypeStruct((rows, d), BF16)] * 2,
        compiler_params=_params("arbitrary"),
        name="mem_kv",
    )(mem.reshape(rows, d), g, w_kv)
    return [o.reshape(batch, n_mem, d) for o in outs]


def _mix_kernel(x_ref, a_ref, p_ref, wo_ref, g_ref, wq_ref, mk_ref, mv_ref, wox_ref, o_ref, *, bb, ts):
    d = x_ref.shape[-1]
    m = bb * ts
    hd = d // N_X_HEADS
    ap = jnp.concatenate([a_ref[...].reshape(m, -1), p_ref[...].reshape(m, -1)], axis=-1)
    x1 = x_ref[...].reshape(m, d) + jnp.dot(ap, wo_ref[...], preferred_element_type=F32)
    h = _rms_bf16(x1, g_ref[...], NORM_EPS)
    q = jnp.dot(h, wq_ref[...], preferred_element_type=F32).astype(BF16)
    rows = []
    for i in range(bb):
        heads = []
        for hh in range(N_X_HEADS):
            cs = slice(hh * hd, (hh + 1) * hd)
            s = lax.dot_general(q[i * ts:(i + 1) * ts, cs], mk_ref[i, :, cs], _NT, preferred_element_type=F32)
            p = jnp.exp(s - jnp.max(s, axis=-1, keepdims=True))
            inv_l = 1.0 / jnp.sum(p, axis=-1, keepdims=True)
            o = jnp.dot(p.astype(BF16), mv_ref[i, :, cs], preferred_element_type=F32) * inv_l
            heads.append(o.astype(BF16))
        rows.append(jnp.concatenate(heads, axis=-1))
    o_all = rows[0] if bb == 1 else jnp.concatenate(rows, axis=0)
    o_ref[...] = (x1 + jnp.dot(o_all, wox_ref[...], preferred_element_type=F32)).reshape(bb, ts, d)


def _mix(x, a, pool_y, w_out, g, wq, mk, mv, wo_x):
    batch, seq, d = x.shape
    bb, ts = _row_tiling(batch, seq, WIDE_ROW_TILE if seq >= WIDE_ROW_TILE else None)
    n_mem = mk.shape[1]
    tok = lambda width: pl.BlockSpec((bb, ts, width), lambda b, s: (b, s, 0))
    mem = pl.BlockSpec((bb, n_mem, d), lambda b, s: (b, 0, 0))
    return pl.pallas_call(
        functools.partial(_mix_kernel, bb=bb, ts=ts),
        grid=(batch // bb, seq // ts),
        in_specs=[tok(d), tok(a.shape[-1]), tok(pool_y.shape[-1]), _const_spec(w_out.shape), _const_spec(g.shape),
                  _const_spec(wq.shape), mem, mem, _const_spec(wo_x.shape)],
        out_specs=tok(d),
        out_shape=jax.ShapeDtypeStruct(x.shape, F32),
        compiler_params=_params("arbitrary", "arbitrary"),
        name="mix",
    )(x, a, pool_y, w_out, g, wq, mk, mv, wo_x)


def _mlp_kernel(x_ref, g_ref, wu_ref, wd_ref, fg_ref, o_ref, *, bb, ts, final):
    d = x_ref.shape[-1]
    m = bb * ts
    x = x_ref[...].reshape(m, d)
    h = _rms_bf16(x, g_ref[...], NORM_EPS)
    ck = wu_ref.shape[1] // MLP_CHUNKS
    acc = x
    for j in range(MLP_CHUNKS):
        up = jnp.maximum(jnp.dot(h, wu_ref[:, j * ck:(j + 1) * ck], preferred_element_type=F32), 0.0)
        acc = acc + jnp.dot((up * up).astype(BF16), wd_ref[j * ck:(j + 1) * ck, :], preferred_element_type=F32)
    if final:
        ms = jnp.mean(acc * acc, axis=-1, keepdims=True)
        acc = acc * lax.rsqrt(ms + NORM_EPS) * fg_ref[...]
    o_ref[...] = acc.reshape(bb, ts, d)


def _mlp(x, g, w_up, w_down, final_g, final):
    batch, seq, d = x.shape
    bb, ts = _row_tiling(batch, seq, WIDE_ROW_TILE if seq >= WIDE_ROW_TILE else None)
    assert w_up.shape[1] % MLP_CHUNKS == 0
    tok = pl.BlockSpec((bb, ts, d), lambda b, s: (b, s, 0))
    return pl.pallas_call(
        functools.partial(_mlp_kernel, bb=bb, ts=ts, final=final),
        grid=(batch // bb, seq // ts),
        in_specs=[tok, _const_spec(g.shape), _const_spec(w_up.shape), _const_spec(w_down.shape),
                  _const_spec(final_g.shape)],
        out_specs=tok,
        out_shape=jax.ShapeDtypeStruct(x.shape, F32),
        compiler_params=_params("arbitrary", "arbitrary"),
        name="mlp",
    )(x, g, w_up, w_down, final_g)


def _head_major(w):
    d = w.shape[0]
    return w.reshape(d, 2, N_DIFF_HEADS, DIFF_DK).transpose(0, 2, 1, 3).reshape(d, DIFF_WIDTH)


def _in_weights(w_in):
    w = DIFF_WIDTH
    wq, wk, wv, wu = w_in[:, :w], w_in[:, w:2 * w], w_in[:, 2 * w:3 * w], w_in[:, 3 * w:]
    wq = _head_major(wq) * (DIFF_DK ** -0.5 * LOG2E)
    cols_p = jnp.concatenate([_head_major(wk), wv, wk, wu], axis=1).astype(BF16)
    rows_t = jnp.concatenate([wq, wv], axis=1).T.astype(BF16)
    cols_s = jnp.concatenate([wq, wv, wk, wu], axis=1).astype(BF16)
    return cols_p, rows_t, cols_s


def kernel(x_prompt, x_sample, cache_k, cache_v, state_pool, cache_mem_k, cache_mem_v, mem_prompt,
           norm_mix_g, w_in, lam_q, lam_k, subln_g, w_pool, pool_scale, w_out,
           norm_x_g, norm_mem_g, wq_x, wk_x, wv_x, wo_x, norm_mlp_g, w_up, w_down, final_g):
    depth = w_in.shape[0]
    batch, seq, d = x_prompt.shape
    dec_batch, dec_seq, _ = x_sample.shape
    past = cache_k.shape[2]
    n_mem = mem_prompt.shape[1]
    hd = d // N_X_HEADS

    ck = cache_k.reshape(depth, dec_batch, past, DIFF_WIDTH)
    cv = cache_v.reshape(depth, dec_batch, past, DIFF_WIDTH)
    cmk = cache_mem_k.reshape(depth, dec_batch, n_mem, d).astype(BF16)
    cmv = cache_mem_v.reshape(depth, dec_batch, n_mem, d).astype(BF16)
    hist_s = jnp.pad(state_pool, ((0, 0), (0, 0), (HIST_ROWS - POOL_HIST, 0), (0, 0)))
    hist_p = jnp.zeros((batch, HIST_ROWS, POOL_WIDTH), F32)
    row = lambda v: v.reshape(1, -1)
    fg = row(final_g)

    xp, xs = x_prompt, x_sample
    outs = [[] for _ in range(6)]
    kv_prompt = None
    for l in range(depth):
        lam_init = 0.8 - 0.6 * math.exp(-0.3 * l)
        last = l == depth - 1
        w_cols_p, w_rows_t, w_cols_s = _in_weights(w_in[l])
        wp = w_pool[l].astype(BF16)
        ps = row(pool_scale[l])
        g_mix, g_x, g_mlp, sg = row(norm_mix_g[l]), row(norm_x_g[l]), row(norm_mlp_g[l]), row(subln_g[l])
        w_o = w_out[l].astype(BF16)
        wq = (wq_x[l] * hd ** -0.5).astype(BF16)
        wo = wo_x[l].astype(BF16)
        wu, wd = w_up[l].astype(BF16), w_down[l].astype(BF16)
        w_kv = jnp.concatenate([wk_x[l], wv_x[l]], axis=1).astype(BF16)

        qt, vta, ka, *kv_prompt, py, ph = _in_proj_prompt(xp, g_mix, w_cols_p, w_rows_t, hist_p, wp, ps, l, depth,
                                                          kv_prompt)
        a = _prompt_attn(qt, ka, vta, lam_q[l], lam_k[l], sg.reshape(-1, 1), lam_init)
        mk, mv, mkb, mvb = _mem_kv(mem_prompt, row(norm_mem_g[l]), w_kv)
        xp = _mix(xp, a, py, w_o, g_x, wq, mkb, mvb, wo)
        xp = _mlp(xp, g_mlp, wu, wd, fg, last)
        for dst, val in zip(outs[:3], (ph[:, HIST_ROWS - POOL_HIST:], mk, mv)):
            dst.append(val)

        q, ko, vo, py, ph = _in_proj_sample(xs, g_mix, w_cols_s, hist_s[l], wp, ps, past)
        a = _sample_attn(q, ko, vo, ck, cv, l, lam_q[l], lam_k[l], sg, lam_init)
        xs = _mix(xs, a, py, w_o, g_x, wq, cmk[l], cmv[l], wo)
        xs = _mlp(xs, g_mlp, wu, wd, fg, last)
        for dst, val in zip(outs[3:], (ko, vo, ph[:, HIST_ROWS - POOL_HIST:])):
            dst.append(val)

    kp, vp = kv_prompt
    pp, mkp, mvp, ks, vs, pps = [jnp.stack(o) for o in outs]
    return (xp, xs,
            kp.reshape(depth, batch, seq, 2, N_DIFF_HEADS, DIFF_DK),
            vp.reshape(depth, batch, seq, N_DIFF_HEADS, DIFF_DV),
            pp,
            mkp.reshape(depth, batch, n_mem, N_X_HEADS, hd),
            mvp.reshape(depth, batch, n_mem, N_X_HEADS, hd),
            ks.reshape(depth, dec_batch, dec_seq, 2, N_DIFF_HEADS, DIFF_DK),
            vs.reshape(depth, dec_batch, dec_seq, N_DIFF_HEADS, DIFF_DV),
            pps)
```

```python
import functools
import math

import jax
import jax.numpy as jnp
from jax import lax
from jax.experimental import pallas as pl
from jax.experimental.pallas import tpu as pltpu

F32 = jnp.float32
BF16 = jnp.bfloat16

CHUNK = 64
CHUNK_SHIFT = CHUNK.bit_length() - 1
N_DIFF_HEADS = 4
DIFF_DK = 64
DIFF_DV = 2 * DIFF_DK
DIFF_WIDTH = N_DIFF_HEADS * DIFF_DV
V_AUG_ROWS = DIFF_DV + 16
POOL_WINDOWS = (2, 4, 8, 16)
POOL_GC = 128
POOL_WIDTH = len(POOL_WINDOWS) * POOL_GC
assert all(w & (w - 1) == 0 for w in POOL_WINDOWS)
POOL_HIST = max(POOL_WINDOWS) - 1
HIST_ROWS = 16
N_X_HEADS = 4
NORM_EPS = 1e-6
SUBLN_EPS = 1e-5
LOG2E = 1.4426950408889634

ROW_TILE = 512
WIDE_ROW_TILE = 1024
MLP_CHUNKS = 4
V7X_VMEM_LIMIT = 56 * 2**20

_NT = (((1,), (1,)), ((), ()))


def _rms_bf16(x, g, eps):
    ms = jnp.mean(x * x, axis=-1, keepdims=True)
    return (x * lax.rsqrt(ms + eps) * g).astype(BF16)


def _row_tiling(batch, seq, rows=None):
    rows = ROW_TILE if rows is None else rows
    if seq >= rows:
        assert seq % rows == 0
        return 1, rows
    bb = max(1, min(batch, rows // seq))
    assert batch % bb == 0 and seq % 8 == 0
    return bb, seq


def _params(*semantics):
    return pltpu.CompilerParams(dimension_semantics=semantics, vmem_limit_bytes=V7X_VMEM_LIMIT)


def _const_spec(shape):
    nd = len(shape)
    return pl.BlockSpec(shape, lambda *_: (0,) * nd, pipeline_mode=pl.Buffered(1))


def _project(x_ref, g_ref, w_ref, ko_ref, vo_ref, ext_ref, hist_ref, ph_ref, bb, ts, start):
    d = x_ref.shape[-1]
    m = bb * ts
    w = DIFF_WIDTH
    h = _rms_bf16(x_ref[...].reshape(m, d), g_ref[...], NORM_EPS)
    u = jnp.dot(h, w_ref[:, 3 * w:], preferred_element_type=F32)

    @pl.when(pl.program_id(1) == 0)
    def _():
        ext_ref[:, 0:HIST_ROWS, :] = hist_ref[...]

    ext_ref[:, HIST_ROWS:HIST_ROWS + ts, :] = u.reshape(bb, ts, POOL_WIDTH)
    groups = _pool_windows(ext_ref, ph_ref, bb, ts, start)
    z = jnp.dot(h, w_ref[:, :3 * w], preferred_element_type=F32)
    vo_ref[...] = z[:, w:2 * w].reshape(bb, ts, w)
    ko_ref[...] = z[:, 2 * w:3 * w].reshape(bb, ts, w)
    return h, z[:, 0:w], groups


def _pool_windows(ext_ref, ph_ref, bb, ts, start):
    pos = start + pl.program_id(1) * ts + lax.broadcasted_iota(jnp.int32, (ts, 1), 0)
    groups = []
    for gi, win_len in enumerate(POOL_WINDOWS):
        cols = slice(gi * POOL_GC, (gi + 1) * POOL_GC)
        inv_cnt = 1.0 / jnp.minimum(win_len, pos + 1).astype(F32)
        pooled = []
        for i in range(bb):
            win = ext_ref[i, :, cols]
            k = 1
            while k < win_len:
                win = win + pltpu.roll(win, k, axis=0)
                k *= 2
            u = ext_ref[i, HIST_ROWS:HIST_ROWS + ts, cols]
            pooled.append((win[HIST_ROWS:] * inv_cnt - u).astype(BF16))
        groups.append(pooled[0] if bb == 1 else jnp.concatenate(pooled, axis=0))
    tail = ext_ref[:, ts:ts + HIST_ROWS, :]
    ph_ref[...] = tail
    ext_ref[:, 0:HIST_ROWS, :] = tail
    return groups


def _pool_project(groups, wp_ref, ps_ref, py_ref, bb, ts):
    for gi, pooled in enumerate(groups):
        cols = slice(gi * POOL_GC, (gi + 1) * POOL_GC)
        y = jnp.dot(pooled, wp_ref[gi], preferred_element_type=F32) * ps_ref[:, cols]
        py_ref[:, :, cols] = y.astype(BF16).reshape(bb, ts, POOL_GC)


def _in_proj_prompt_kernel(x_ref, g_ref, w_ref, wt_ref, hist_ref, wp_ref, ps_ref, *refs, ts, start):
    qt_ref, vta_ref, ka_ref, ko_ref, vo_ref, py_ref, ph_ref, ext_ref = refs[-8:]
    if len(refs) == 8:
        for ref in (ko_ref, vo_ref):
            if ref.shape[0] > 1:
                ref[1:] = jnp.zeros((ref.shape[0] - 1,) + ref.shape[1:], F32)
        ko_ref, vo_ref = ko_ref.at[0], vo_ref.at[0]
    h, k_hm, groups = _project(x_ref, g_ref, w_ref, ko_ref, vo_ref, ext_ref, hist_ref, ph_ref, 1, ts, start)
    ka_ref[...] = k_hm.astype(BF16)
    _pool_project(groups, wp_ref, ps_ref, py_ref, 1, ts)
    zt = lax.dot_general(wt_ref[...], h, _NT, preferred_element_type=F32).astype(BF16)
    w = DIFF_WIDTH
    qt_ref[...] = zt[:w]
    ones = (lax.broadcasted_iota(jnp.int32, (V_AUG_ROWS - DIFF_DV, ts), 0) == 0).astype(BF16)
    for hd in range(N_DIFF_HEADS):
        vta_ref[hd * V_AUG_ROWS:hd * V_AUG_ROWS + DIFF_DV, :] = zt[w + hd * DIFF_DV:w + (hd + 1) * DIFF_DV]
        vta_ref[hd * V_AUG_ROWS + DIFF_DV:(hd + 1) * V_AUG_ROWS, :] = ones


def _in_proj_sample_kernel(x_ref, g_ref, w_ref, hist_ref, wp_ref, ps_ref,
                           q_ref, ko_ref, vo_ref, py_ref, ph_ref, ext_ref, *, bb, ts, start):
    _, q, groups = _project(x_ref, g_ref, w_ref, ko_ref, vo_ref, ext_ref, hist_ref, ph_ref, bb, ts, start)
    q_ref[...] = q.astype(BF16).reshape(bb, ts, DIFF_WIDTH)
    _pool_project(groups, wp_ref, ps_ref, py_ref, bb, ts)


def _in_proj_common(batch, seq, bb, ts):
    assert seq >= HIST_ROWS
    tok = lambda width: pl.BlockSpec((bb, ts, width), lambda b, s: (b, s, 0))
    hist_spec = pl.BlockSpec((bb, HIST_ROWS, POOL_WIDTH), lambda b, s: (b, 0, 0))
    tok_shape = lambda width, dt: jax.ShapeDtypeStruct((batch, seq, width), dt)
    pool_specs = [tok(POOL_WIDTH), hist_spec]
    pool_shapes = [tok_shape(POOL_WIDTH, BF16), jax.ShapeDtypeStruct((batch, HIST_ROWS, POOL_WIDTH), F32)]
    return tok, hist_spec, tok_shape, pool_specs, pool_shapes


def _in_proj_prompt(x, g, w_cols, w_rows_t, hist, w_pool, pool_scale, layer, depth, kv_all):
    batch, seq, d = x.shape
    bb, ts = _row_tiling(batch, seq)
    assert bb == 1
    w = DIFF_WIDTH
    tok, hist_spec, tok_shape, pool_specs, pool_shapes = _in_proj_common(batch, seq, bb, ts)
    va_rows = N_DIFF_HEADS * V_AUG_ROWS
    tile_t = lambda rows: pl.BlockSpec((None, None, rows, ts), lambda b, s: (b, s, 0, 0))
    tile_t_shape = lambda rows: jax.ShapeDtypeStruct((batch, seq // ts, rows, ts), BF16)
    if kv_all is None:
        assert layer == 0
        layered = pl.BlockSpec((depth, bb, ts, w), lambda b, s: (0, b, s, 0))
    else:
        layered = pl.BlockSpec((None, bb, ts, w), lambda b, s: (layer, b, s, 0))
    in_specs = [tok(d), _const_spec(g.shape), _const_spec(w_cols.shape), _const_spec(w_rows_t.shape),
                hist_spec, _const_spec(w_pool.shape), _const_spec(pool_scale.shape)]
    args = [x, g, w_cols, w_rows_t, hist, w_pool, pool_scale]
    aliases = {}
    if kv_all is not None:
        aliases = {len(args): 3, len(args) + 1: 4}
        in_specs += [pl.BlockSpec(memory_space=pl.ANY)] * 2
        args += list(kv_all)
    return pl.pallas_call(
        functools.partial(_in_proj_prompt_kernel, ts=ts, start=0),
        grid=(batch // bb, seq // ts),
        in_specs=in_specs,
        out_specs=[tile_t(w), tile_t(va_rows),
                   pl.BlockSpec((None, ts, w), lambda b, s: (b, s, 0)), layered, layered] + pool_specs,
        out_shape=[tile_t_shape(w), tile_t_shape(va_rows), tok_shape(w, BF16),
                   jax.ShapeDtypeStruct((depth, batch, seq, w), F32),
                   jax.ShapeDtypeStruct((depth, batch, seq, w), F32)] + pool_shapes,
        input_output_aliases=aliases,
        scratch_shapes=[pltpu.VMEM((bb, HIST_ROWS + ts, POOL_WIDTH), F32)],
        compiler_params=_params("arbitrary", "arbitrary"),
        name="in_proj_prompt",
    )(*args)


def _in_proj_sample(x, g, w_cols, hist, w_pool, pool_scale, start):
    batch, seq, d = x.shape
    bb, ts = _row_tiling(batch, seq)
    w = DIFF_WIDTH
    tok, hist_spec, tok_shape, pool_specs, pool_shapes = _in_proj_common(batch, seq, bb, ts)
    return pl.pallas_call(
        functools.partial(_in_proj_sample_kernel, bb=bb, ts=ts, start=start),
        grid=(batch // bb, seq // ts),
        in_specs=[tok(d), _const_spec(g.shape), _const_spec(w_cols.shape),
                  hist_spec, _const_spec(w_pool.shape), _const_spec(pool_scale.shape)],
        out_specs=[tok(w), tok(w), tok(w)] + pool_specs,
        out_shape=[tok_shape(w, BF16), tok_shape(w, F32), tok_shape(w, F32)] + pool_shapes,
        scratch_shapes=[pltpu.VMEM((bb, HIST_ROWS + ts, POOL_WIDTH), F32)],
        compiler_params=_params("arbitrary", "arbitrary"),
        name="in_proj_sample",
    )(x, g, w_cols, hist, w_pool, pool_scale)


def _lambda(lq_ref, lk_ref, lam_init):
    e = jnp.exp(jnp.sum(lq_ref[...] * lk_ref[...], axis=-1, keepdims=True))
    return e[0:1] - e[1:2] + lam_init


def _prompt_attn_kernel(qt_ref, k_ref, vt_ref, lq_ref, lk_ref, sg_ref, o_ref, s_ref, acc_ref, *, t, n_blk, lam_init):
    lam = _lambda(lq_ref, lk_ref, lam_init)

    def masked_q(i, mp):
        zero = jnp.zeros((DIFF_DK, t), BF16)
        if mp == 0:
            return jnp.concatenate([qt_ref[i, :DIFF_DK, :], zero], axis=0)
        return jnp.concatenate([zero, qt_ref[i, DIFF_DK:, :]], axis=0)

    def scores(i, j, slot, mp):
        k_blk = k_ref[pl.ds(pl.multiple_of(j * t, t), t), :]
        s_ref[slot, mp] = jnp.dot(k_blk, masked_q(i, mp), preferred_element_type=F32)

    def update(j, slot, mp, m, masked):
        s = s_ref[slot, mp]
        if masked:
            key = lax.broadcasted_iota(jnp.int32, s.shape, 0)
            qry = lax.broadcasted_iota(jnp.int32, s.shape, 1)
            s = jnp.where(lax.shift_right_logical(key, CHUNK_SHIFT) <= lax.shift_right_logical(qry, CHUNK_SHIFT),
                          s, -jnp.inf)
        m_new = jnp.maximum(m, jnp.max(s, axis=0, keepdims=True))
        alpha = jnp.exp2(m - m_new)
        p = jnp.exp2(s - m_new)
        acc_ref[mp] = alpha * acc_ref[mp] + jnp.dot(vt_ref[j], p.astype(BF16), preferred_element_type=F32)
        return m_new

    def step(i, j, cur, nxt, m0, m1):
        scores(i, j + 1, nxt, 0)
        m0 = update(j, cur, 0, m0, False)
        scores(i, j + 1, nxt, 1)
        m1 = update(j, cur, 1, m1, False)
        return m0, m1

    def finish(i, slot, m0, m1, next_first):
        if next_first is not None:
            scores(i + 1, 0, next_first, 0)
        update(i, slot, 0, m0, True)
        if next_first is not None:
            scores(i + 1, 0, next_first, 1)
        update(i, slot, 1, m1, True)
        normalised = lambda acc: acc[:DIFF_DV] * (1.0 / acc[DIFF_DV:DIFF_DV + 1])
        o = normalised(acc_ref[0]) - lam * normalised(acc_ref[1])
        ms = jnp.mean(o * o, axis=0, keepdims=True)
        y = o * lax.rsqrt(ms + SUBLN_EPS) * sg_ref[...] * (1.0 - lam_init)
        o_ref[pl.ds(pl.multiple_of(i * t, t), t), :] = y.T.astype(BF16)

    def start():
        acc_ref[...] = jnp.zeros(acc_ref.shape, F32)
        return jnp.full((1, t), -jnp.inf, F32)

    def query_block(i, odd, next_first):
        init = start()
        m0, m1 = step(i, 0, 3 if odd else 2, 0, init, init)
        pair = lambda j, c: step(i, j + 1, 1, 0, *step(i, j, 0, 1, *c))
        n_quads = (i - 1) // 4
        m0, m1 = lax.fori_loop(0, n_quads, lambda p, c: pair(4 * p + 3, pair(4 * p + 1, c)), (m0, m1))
        m0, m1 = lax.cond((i - 1) % 4 >= 2, lambda c: pair(4 * n_quads + 1, c), lambda c: c, (m0, m1))
        if odd:
            finish(i, 0, m0, m1, next_first)
        else:
            finish(i, 1, *step(i, i - 1, 0, 1, m0, m1), next_first)

    scores(0, 0, 2, 0)
    scores(0, 0, 2, 1)
    init = start()
    finish(0, 2, init, init, 3 if n_blk > 1 else None)
    if n_blk > 1:
        def two_query_blocks(ip, carry):
            query_block(2 * ip + 1, True, 2)
            query_block(2 * ip + 2, False, 3)
            return carry

        lax.fori_loop(0, (n_blk - 2) // 2, two_query_blocks, 0)
        query_block(n_blk - 1, True, None)


def _prompt_attn(qt, k, vta, lam_q, lam_k, subln_g_col, lam_init):
    batch, n_blk, _, t = qt.shape
    seq = n_blk * t
    assert t % CHUNK == 0 and (t & (t - 1)) == 0
    assert n_blk == 1 or n_blk % 2 == 0
    return pl.pallas_call(
        functools.partial(_prompt_attn_kernel, t=t, n_blk=n_blk, lam_init=lam_init),
        grid=(batch, N_DIFF_HEADS),
        in_specs=[pl.BlockSpec((None, n_blk, DIFF_DV, t), lambda b, h: (b, 0, h, 0)),
                  pl.BlockSpec((None, seq, DIFF_DV), lambda b, h: (b, 0, h)),
                  pl.BlockSpec((None, n_blk, V_AUG_ROWS, t), lambda b, h: (b, 0, h, 0)),
                  _const_spec(lam_q.shape), _const_spec(lam_k.shape), _const_spec(subln_g_col.shape)],
        out_specs=pl.BlockSpec((None, seq, DIFF_DV), lambda b, h: (b, 0, h)),
        out_shape=jax.ShapeDtypeStruct((batch, seq, DIFF_WIDTH), BF16),
        scratch_shapes=[pltpu.VMEM((4, 2, t, t), F32),
                        pltpu.VMEM((2, V_AUG_ROWS, t), F32)],
        compiler_params=_params("arbitrary", "arbitrary"),
        name="prompt_attn",
    )(qt, k, vta, lam_q, lam_k, subln_g_col)


def _sample_attn_kernel(q_ref, kn_ref, vn_ref, ck_ref, cv_ref, lq_ref, lk_ref, sg_ref, o_ref,
                        *, past, n_new, lam_init):
    lam = _lambda(lq_ref, lk_ref, lam_init)
    sg = sg_ref[...]
    lane = lax.broadcasted_iota(jnp.int32, (n_new, DIFF_DV), 1)
    r = lax.broadcasted_iota(jnp.int32, (n_new, n_new), 0)
    c = lax.broadcasted_iota(jnp.int32, (n_new, n_new), 1)
    visible = lax.shift_right_logical(past + c, CHUNK_SHIFT) <= lax.shift_right_logical(past + r, CHUNK_SHIFT)
    visible2 = jnp.concatenate([visible, visible], axis=0)
    head_cols = lambda h: slice(h * DIFF_DV, (h + 1) * DIFF_DV)

    def masked_q(h, mp):
        q_blk = q_ref[:, head_cols(h)].astype(F32)
        qm = q_blk if mp == h % 2 else pltpu.roll(q_blk, DIFF_DK, axis=1)
        return jnp.where((lane >= DIFF_DK) == bool(h % 2), qm, 0.0).astype(BF16)

    v_c = [cv_ref[:, head_cols(h)].astype(BF16) for h in range(N_DIFF_HEADS)]
    v_n = [vn_ref[:, head_cols(h)].astype(BF16) for h in range(N_DIFF_HEADS)]
    o_maps = {}
    for mp in range(2):
        for pair in range(N_DIFF_HEADS // 2):
            heads = (2 * pair, 2 * pair + 1)
            ks = head_cols(2 * mp + pair)
            qq = jnp.concatenate([masked_q(h, mp) for h in heads], axis=0)
            s_c = lax.dot_general(qq, ck_ref[:, ks].astype(BF16), _NT, preferred_element_type=F32)
            s_n = lax.dot_general(qq, kn_ref[:, ks].astype(BF16), _NT, preferred_element_type=F32)
            s_n = jnp.where(visible2, s_n, -jnp.inf)
            mx = jnp.maximum(jnp.max(s_c, axis=-1, keepdims=True), jnp.max(s_n, axis=-1, keepdims=True))
            p_c = jnp.exp2(s_c - mx)
            p_n = jnp.exp2(s_n - mx)
            inv_l = 1.0 / (jnp.sum(p_c, axis=-1, keepdims=True) + jnp.sum(p_n, axis=-1, keepdims=True))
            p_c, p_n = p_c.astype(BF16), p_n.astype(BF16)
            for n, h in enumerate(heads):
                rows = slice(n * n_new, (n + 1) * n_new)
                o = jnp.dot(p_c[rows], v_c[h], preferred_element_type=F32) + \
                    jnp.dot(p_n[rows], v_n[h], preferred_element_type=F32)
                o_maps[h, mp] = o * inv_l[rows]
    for h in range(N_DIFF_HEADS):
        o = o_maps[h, 0] - lam * o_maps[h, 1]
        ms = jnp.mean(o * o, axis=-1, keepdims=True)
        o_ref[:, head_cols(h)] = (o * lax.rsqrt(ms + SUBLN_EPS) * sg * (1.0 - lam_init)).astype(BF16)


def _sample_attn(q, k_new, v_new, cache_k, cache_v, layer, lam_q, lam_k, subln_g, lam_init):
    batch, n_new, _ = q.shape
    past = cache_k.shape[2]
    w = DIFF_WIDTH
    new = pl.BlockSpec((None, n_new, w), lambda b: (b, 0, 0))
    cached = pl.BlockSpec((None, None, past, w), lambda b: (layer, b, 0, 0))
    return pl.pallas_call(
        functools.partial(_sample_attn_kernel, past=past, n_new=n_new, lam_init=lam_init),
        grid=(batch,),
        in_specs=[new, new, new, cached, cached,
                  _const_spec(lam_q.shape), _const_spec(lam_k.shape), _const_spec(subln_g.shape)],
        out_specs=new,
        out_shape=jax.ShapeDtypeStruct((batch, n_new, w), BF16),
        compiler_params=_params("arbitrary"),
        name="sample_attn",
    )(q, k_new, v_new, cache_k, cache_v, lam_q, lam_k, subln_g)


def _mem_kv_kernel(m_ref, g_ref, w_ref, k_ref, v_ref, kb_ref, vb_ref):
    d = m_ref.shape[-1]
    z = jnp.dot(_rms_bf16(m_ref[...], g_ref[...], NORM_EPS), w_ref[...], preferred_element_type=F32)
    k_ref[...] = z[:, :d]
    v_ref[...] = z[:, d:]
    kb_ref[...] = z[:, :d].astype(BF16)
    vb_ref[...] = z[:, d:].astype(BF16)


def _mem_kv(mem, g, w_kv):
    batch, n_mem, d = mem.shape
    rows = batch * n_mem
    tm = min(ROW_TILE, rows)
    assert rows % tm == 0
    tok = pl.BlockSpec((tm, d), lambda i: (i, 0))
    outs = pl.pallas_call(
        _mem_kv_kernel,
        grid=(rows // tm,),
        in_specs=[tok, _const_spec(g.shape), _const_spec(w_kv.shape)],
        out_specs=[tok] * 4,
        out_shape=[jax.ShapeDtypeStruct((rows, d), F32)] * 2 + [jax.ShapeDtypeStruct((rows, d), BF16)] * 2,
        compiler_params=_params("arbitrary"),
        name="mem_kv",
    )(mem.reshape(rows, d), g, w_kv)
    return [o.reshape(batch, n_mem, d) for o in outs]


def _mix_kernel(x_ref, a_ref, p_ref, wo_ref, g_ref, wq_ref, mk_ref, mv_ref, wox_ref, o_ref, *, bb, ts):
    d = x_ref.shape[-1]
    m = bb * ts
    hd = d // N_X_HEADS
    ap = jnp.concatenate([a_ref[...].reshape(m, -1), p_ref[...].reshape(m, -1)], axis=-1)
    x1 = x_ref[...].reshape(m, d) + jnp.dot(ap, wo_ref[...], preferred_element_type=F32)
    h = _rms_bf16(x1, g_ref[...], NORM_EPS)
    q = jnp.dot(h, wq_ref[...], preferred_element_type=F32).astype(BF16)
    units = [(i, hh) for i in range(bb) for hh in range(N_X_HEADS)]
    scores = lambda i, hh: lax.dot_general(q[i * ts:(i + 1) * ts, hh * hd:(hh + 1) * hd],
                                           mk_ref[i, :, hh * hd:(hh + 1) * hd], _NT, preferred_element_type=F32)
    outs = []
    s_next = scores(*units[0])
    for n, (i, hh) in enumerate(units):
        s = s_next
        if n + 1 < len(units):
            s_next = scores(*units[n + 1])
        p = jnp.exp(s - jnp.max(s, axis=-1, keepdims=True))
        inv_l = 1.0 / jnp.sum(p, axis=-1, keepdims=True)
        o = jnp.dot(p.astype(BF16), mv_ref[i, :, hh * hd:(hh + 1) * hd], preferred_element_type=F32) * inv_l
        outs.append(o.astype(BF16))
    rows = [jnp.concatenate(outs[i * N_X_HEADS:(i + 1) * N_X_HEADS], axis=-1) for i in range(bb)]
    o_all = rows[0] if bb == 1 else jnp.concatenate(rows, axis=0)
    o_ref[...] = (x1 + jnp.dot(o_all, wox_ref[...], preferred_element_type=F32)).reshape(bb, ts, d)


def _mix(x, a, pool_y, w_out, g, wq, mk, mv, wo_x):
    batch, seq, d = x.shape
    bb, ts = _row_tiling(batch, seq, WIDE_ROW_TILE if seq >= WIDE_ROW_TILE else None)
    n_mem = mk.shape[1]
    tok = lambda width: pl.BlockSpec((bb, ts, width), lambda b, s: (b, s, 0))
    mem = pl.BlockSpec((bb, n_mem, d), lambda b, s: (b, 0, 0))
    return pl.pallas_call(
        functools.partial(_mix_kernel, bb=bb, ts=ts),
        grid=(batch // bb, seq // ts),
        in_specs=[tok(d), tok(a.shape[-1]), tok(pool_y.shape[-1]), _const_spec(w_out.shape), _const_spec(g.shape),
                  _const_spec(wq.shape), mem, mem, _const_spec(wo_x.shape)],
        out_specs=tok(d),
        out_shape=jax.ShapeDtypeStruct(x.shape, F32),
        compiler_params=_params("arbitrary", "arbitrary"),
        name="mix",
    )(x, a, pool_y, w_out, g, wq, mk, mv, wo_x)


def _mlp_kernel(x_ref, g_ref, wu_ref, wd_ref, fg_ref, o_ref, *, bb, ts, final):
    d = x_ref.shape[-1]
    m = bb * ts
    x = x_ref[...].reshape(m, d)
    h = _rms_bf16(x, g_ref[...], NORM_EPS)
    ck = wu_ref.shape[1] // MLP_CHUNKS
    acc = x
    for j in range(MLP_CHUNKS):
        up = jnp.maximum(jnp.dot(h, wu_ref[:, j * ck:(j + 1) * ck], preferred_element_type=F32), 0.0)
        acc = acc + jnp.dot((up * up).astype(BF16), wd_ref[j * ck:(j + 1) * ck, :], preferred_element_type=F32)
    if final:
        ms = jnp.mean(acc * acc, axis=-1, keepdims=True)
        acc = acc * lax.rsqrt(ms + NORM_EPS) * fg_ref[...]
    o_ref[...] = acc.reshape(bb, ts, d)


def _mlp(x, g, w_up, w_down, final_g, final):
    batch, seq, d = x.shape
    bb, ts = _row_tiling(batch, seq, WIDE_ROW_TILE if seq >= WIDE_ROW_TILE else None)
    assert w_up.shape[1] % MLP_CHUNKS == 0
    tok = pl.BlockSpec((bb, ts, d), lambda b, s: (b, s, 0))
    return pl.pallas_call(
        functools.partial(_mlp_kernel, bb=bb, ts=ts, final=final),
        grid=(batch // bb, seq // ts),
        in_specs=[tok, _const_spec(g.shape), _const_spec(w_up.shape), _const_spec(w_down.shape),
                  _const_spec(final_g.shape)],
        out_specs=tok,
        out_shape=jax.ShapeDtypeStruct(x.shape, F32),
        compiler_params=_params("arbitrary", "arbitrary"),
        name="mlp",
    )(x, g, w_up, w_down, final_g)


def _head_major(w):
    d = w.shape[0]
    return w.reshape(d, 2, N_DIFF_HEADS, DIFF_DK).transpose(0, 2, 1, 3).reshape(d, DIFF_WIDTH)


def _in_weights(w_in):
    w = DIFF_WIDTH
    wq, wk, wv, wu = w_in[:, :w], w_in[:, w:2 * w], w_in[:, 2 * w:3 * w], w_in[:, 3 * w:]
    wq = _head_major(wq) * (DIFF_DK ** -0.5 * LOG2E)
    cols_p = jnp.concatenate([_head_major(wk), wv, wk, wu], axis=1).astype(BF16)
    rows_t = jnp.concatenate([wq, wv], axis=1).T.astype(BF16)
    cols_s = jnp.concatenate([wq, wv, wk, wu], axis=1).astype(BF16)
    return cols_p, rows_t, cols_s


def kernel(x_prompt, x_sample, cache_k, cache_v, state_pool, cache_mem_k, cache_mem_v, mem_prompt,
           norm_mix_g, w_in, lam_q, lam_k, subln_g, w_pool, pool_scale, w_out,
           norm_x_g, norm_mem_g, wq_x, wk_x, wv_x, wo_x, norm_mlp_g, w_up, w_down, final_g):
    depth = w_in.shape[0]
    batch, seq, d = x_prompt.shape
    dec_batch, dec_seq, _ = x_sample.shape
    past = cache_k.shape[2]
    n_mem = mem_prompt.shape[1]
    hd = d // N_X_HEADS

    ck = cache_k.reshape(depth, dec_batch, past, DIFF_WIDTH)
    cv = cache_v.reshape(depth, dec_batch, past, DIFF_WIDTH)
    cmk = cache_mem_k.reshape(depth, dec_batch, n_mem, d).astype(BF16)
    cmv = cache_mem_v.reshape(depth, dec_batch, n_mem, d).astype(BF16)
    hist_s = jnp.pad(state_pool, ((0, 0), (0, 0), (HIST_ROWS - POOL_HIST, 0), (0, 0)))
    hist_p = jnp.zeros((batch, HIST_ROWS, POOL_WIDTH), F32)
    row = lambda v: v.reshape(1, -1)
    fg = row(final_g)

    xp, xs = x_prompt, x_sample
    outs = [[] for _ in range(6)]
    kv_prompt = None
    for l in range(depth):
        lam_init = 0.8 - 0.6 * math.exp(-0.3 * l)
        last = l == depth - 1
        w_cols_p, w_rows_t, w_cols_s = _in_weights(w_in[l])
        wp = w_pool[l].astype(BF16)
        ps = row(pool_scale[l])
        g_mix, g_x, g_mlp, sg = row(norm_mix_g[l]), row(norm_x_g[l]), row(norm_mlp_g[l]), row(subln_g[l])
        w_o = w_out[l].astype(BF16)
        wq = (wq_x[l] * hd ** -0.5).astype(BF16)
        wo = wo_x[l].astype(BF16)
        wu, wd = w_up[l].astype(BF16), w_down[l].astype(BF16)
        w_kv = jnp.concatenate([wk_x[l], wv_x[l]], axis=1).astype(BF16)

        qt, vta, ka, *kv_prompt, py, ph = _in_proj_prompt(xp, g_mix, w_cols_p, w_rows_t, hist_p, wp, ps, l, depth,
                                                          kv_prompt)
        a = _prompt_attn(qt, ka, vta, lam_q[l], lam_k[l], sg.reshape(-1, 1), lam_init)
        mk, mv, mkb, mvb = _mem_kv(mem_prompt, row(norm_mem_g[l]), w_kv)
        xp = _mix(xp, a, py, w_o, g_x, wq, mkb, mvb, wo)
        xp = _mlp(xp, g_mlp, wu, wd, fg, last)
        for dst, val in zip(outs[:3], (ph[:, HIST_ROWS - POOL_HIST:], mk, mv)):
            dst.append(val)

        q, ko, vo, py, ph = _in_proj_sample(xs, g_mix, w_cols_s, hist_s[l], wp, ps, past)
        a = _sample_attn(q, ko, vo, ck, cv, l, lam_q[l], lam_k[l], sg, lam_init)
        xs = _mix(xs, a, py, w_o, g_x, wq, cmk[l], cmv[l], wo)
        xs = _mlp(xs, g_mlp, wu, wd, fg, last)
        for dst, val in zip(outs[3:], (ko, vo, ph[:, HIST_ROWS - POOL_HIST:])):
            dst.append(val)

    kp, vp = kv_prompt
    pp, mkp, mvp, ks, vs, pps = [jnp.stack(o) for o in outs]
    return (xp, xs,
            kp.reshape(depth, batch, seq, 2, N_DIFF_HEADS, DIFF_DK),
            vp.reshape(depth, batch, seq, N_DIFF_HEADS, DIFF_DV),
            pp,
            mkp.reshape(depth, batch, n_mem, N_X_HEADS, hd),
            mvp.reshape(depth, batch, n_mem, N_X_HEADS, hd),
            ks.reshape(depth, dec_batch, dec_seq, 2, N_DIFF_HEADS, DIFF_DK),
            vs.reshape(depth, dec_batch, dec_seq, N_DIFF_HEADS, DIFF_DV),
            pps)
```

```python
import functools
import math

import jax
import jax.numpy as jnp
from jax import lax
from jax.experimental import pallas as pl
from jax.experimental.pallas import tpu as pltpu

F32 = jnp.float32
BF16 = jnp.bfloat16

CHUNK = 64
CHUNK_SHIFT = CHUNK.bit_length() - 1
N_DIFF_HEADS = 4
DIFF_DK = 64
DIFF_DV = 2 * DIFF_DK
DIFF_WIDTH = N_DIFF_HEADS * DIFF_DV
V_AUG_ROWS = DIFF_DV + 16
POOL_WINDOWS = (2, 4, 8, 16)
POOL_GC = 128
POOL_WIDTH = len(POOL_WINDOWS) * POOL_GC
assert all(w & (w - 1) == 0 for w in POOL_WINDOWS)
POOL_HIST = max(POOL_WINDOWS) - 1
HIST_ROWS = 16
N_X_HEADS = 4
NORM_EPS = 1e-6
SUBLN_EPS = 1e-5
LOG2E = 1.4426950408889634

ROW_TILE = 512
WIDE_ROW_TILE = 1024
MLP_CHUNKS = 4
V7X_VMEM_LIMIT = 56 * 2**20

_NT = (((1,), (1,)), ((), ()))


def _rms_bf16(x, g, eps):
    ms = jnp.mean(x * x, axis=-1, keepdims=True)
    return (x * lax.rsqrt(ms + eps) * g).astype(BF16)


def _row_tiling(batch, seq, rows=None):
    rows = ROW_TILE if rows is None else rows
    if seq >= rows:
        assert seq % rows == 0
        return 1, rows
    bb = max(1, min(batch, rows // seq))
    assert batch % bb == 0 and seq % 8 == 0
    return bb, seq


def _params(*semantics):
    return pltpu.CompilerParams(dimension_semantics=semantics, vmem_limit_bytes=V7X_VMEM_LIMIT)


def _const_spec(shape):
    nd = len(shape)
    return pl.BlockSpec(shape, lambda *_: (0,) * nd, pipeline_mode=pl.Buffered(1))


def _project(x_ref, g_ref, w_ref, ko_ref, vo_ref, ext_ref, hist_ref, ph_ref, wp_ref, ps_ref, py_ref, bb, ts, start):
    d = x_ref.shape[-1]
    m = bb * ts
    w = DIFF_WIDTH
    h = _rms_bf16(x_ref[...].reshape(m, d), g_ref[...], NORM_EPS)
    u = jnp.dot(h, w_ref[:, w_ref.shape[1] - POOL_WIDTH:], preferred_element_type=F32)

    @pl.when(pl.program_id(1) == 0)
    def _():
        ext_ref[:, 0:HIST_ROWS, :] = hist_ref[...]

    ext_ref[:, HIST_ROWS:HIST_ROWS + ts, :] = u.reshape(bb, ts, POOL_WIDTH)
    groups = _pool_windows(ext_ref, ph_ref, bb, ts, start)
    half = len(groups) // 2
    n_lead = w_ref.shape[1] - 2 * w - POOL_WIDTH
    first = jnp.dot(h, w_ref[:, :n_lead], preferred_element_type=F32) if n_lead else None
    _pool_project(groups[:half], 0, wp_ref, ps_ref, py_ref, bb, ts)
    v = jnp.dot(h, w_ref[:, n_lead:n_lead + w], preferred_element_type=F32)
    vo_ref[...] = v.reshape(bb, ts, w)
    _pool_project(groups[half:], half, wp_ref, ps_ref, py_ref, bb, ts)
    k = jnp.dot(h, w_ref[:, n_lead + w:n_lead + 2 * w], preferred_element_type=F32)
    ko_ref[...] = k.reshape(bb, ts, w)
    return h, first, v, k


def _pool_windows(ext_ref, ph_ref, bb, ts, start):
    pos = start + pl.program_id(1) * ts + lax.broadcasted_iota(jnp.int32, (ts, 1), 0)
    groups = []
    for gi, win_len in enumerate(POOL_WINDOWS):
        cols = slice(gi * POOL_GC, (gi + 1) * POOL_GC)
        inv_cnt = 1.0 / jnp.minimum(win_len, pos + 1).astype(F32)
        pooled = []
        for i in range(bb):
            win = ext_ref[i, :, cols]
            k = 1
            while k < win_len:
                win = win + pltpu.roll(win, k, axis=0)
                k *= 2
            u = ext_ref[i, HIST_ROWS:HIST_ROWS + ts, cols]
            pooled.append((win[HIST_ROWS:] * inv_cnt - u).astype(BF16))
        groups.append(pooled[0] if bb == 1 else jnp.concatenate(pooled, axis=0))
    tail = ext_ref[:, ts:ts + HIST_ROWS, :]
    ph_ref[...] = tail
    ext_ref[:, 0:HIST_ROWS, :] = tail
    return groups


def _pool_project(groups, first_group, wp_ref, ps_ref, py_ref, bb, ts):
    for gi, pooled in enumerate(groups, first_group):
        cols = slice(gi * POOL_GC, (gi + 1) * POOL_GC)
        y = jnp.dot(pooled, wp_ref[gi], preferred_element_type=F32) * ps_ref[:, cols]
        py_ref[:, :, cols] = y.astype(BF16).reshape(bb, ts, POOL_GC)


def _in_proj_prompt_kernel(x_ref, g_ref, w_ref, wt_ref, hist_ref, wp_ref, ps_ref, *refs, ts, start):
    qt_ref, vta_ref, ka_ref, ko_ref, vo_ref, py_ref, ph_ref, ext_ref = refs[-8:]
    if len(refs) == 8:
        for ref in (ko_ref, vo_ref):
            if ref.shape[0] > 1:
                ref[1:] = jnp.zeros((ref.shape[0] - 1,) + ref.shape[1:], F32)
        ko_ref, vo_ref = ko_ref.at[0], vo_ref.at[0]
    h, _, v, k = _project(x_ref, g_ref, w_ref, ko_ref, vo_ref, ext_ref, hist_ref, ph_ref, wp_ref, ps_ref, py_ref,
                       1, ts, start)
    blocks = [k[:, j * DIFF_DV:(j + 1) * DIFF_DV] for j in range(DIFF_WIDTH // DIFF_DV)]
    low_half = lax.broadcasted_iota(jnp.int32, (ts, DIFF_DV), 1) < DIFF_DK
    for hd in range(N_DIFF_HEADS):
        map0, map1 = blocks[hd // 2], blocks[N_DIFF_HEADS // 2 + hd // 2]
        if hd % 2 == 0:
            k_hm = jnp.where(low_half, map0, pltpu.roll(map1, DIFF_DK, axis=1))
        else:
            k_hm = jnp.where(low_half, pltpu.roll(map0, DIFF_DK, axis=1), map1)
        ka_ref[:, hd * DIFF_DV:(hd + 1) * DIFF_DV] = k_hm.astype(BF16)
    qt_ref[...] = lax.dot_general(wt_ref[...], h, _NT, preferred_element_type=F32).astype(BF16)
    ones = (lax.broadcasted_iota(jnp.int32, (V_AUG_ROWS - DIFF_DV, ts), 0) == 0).astype(BF16)
    for hd in range(N_DIFF_HEADS):
        vta_ref[hd * V_AUG_ROWS:hd * V_AUG_ROWS + DIFF_DV, :] = v[:, hd * DIFF_DV:(hd + 1) * DIFF_DV].T.astype(BF16)
        vta_ref[hd * V_AUG_ROWS + DIFF_DV:(hd + 1) * V_AUG_ROWS, :] = ones


def _in_proj_sample_kernel(x_ref, g_ref, w_ref, hist_ref, wp_ref, ps_ref,
                           q_ref, ko_ref, vo_ref, py_ref, ph_ref, ext_ref, *, bb, ts, start):
    _, q, _, _ = _project(x_ref, g_ref, w_ref, ko_ref, vo_ref, ext_ref, hist_ref, ph_ref, wp_ref, ps_ref, py_ref,
                       bb, ts, start)
    q_ref[...] = q.astype(BF16).reshape(bb, ts, DIFF_WIDTH)


def _in_proj_common(batch, seq, bb, ts):
    assert seq >= HIST_ROWS
    tok = lambda width: pl.BlockSpec((bb, ts, width), lambda b, s: (b, s, 0))
    hist_spec = pl.BlockSpec((bb, HIST_ROWS, POOL_WIDTH), lambda b, s: (b, 0, 0))
    tok_shape = lambda width, dt: jax.ShapeDtypeStruct((batch, seq, width), dt)
    pool_specs = [tok(POOL_WIDTH), hist_spec]
    pool_shapes = [tok_shape(POOL_WIDTH, BF16), jax.ShapeDtypeStruct((batch, HIST_ROWS, POOL_WIDTH), F32)]
    return tok, hist_spec, tok_shape, pool_specs, pool_shapes


def _in_proj_prompt(x, g, w_cols, w_rows_t, hist, w_pool, pool_scale, layer, depth, kv_all):
    batch, seq, d = x.shape
    bb, ts = _row_tiling(batch, seq)
    assert bb == 1
    w = DIFF_WIDTH
    tok, hist_spec, tok_shape, pool_specs, pool_shapes = _in_proj_common(batch, seq, bb, ts)
    va_rows = N_DIFF_HEADS * V_AUG_ROWS
    tile_t = lambda rows: pl.BlockSpec((None, None, rows, ts), lambda b, s: (b, s, 0, 0))
    tile_t_shape = lambda rows: jax.ShapeDtypeStruct((batch, seq // ts, rows, ts), BF16)
    if kv_all is None:
        assert layer == 0
        layered = pl.BlockSpec((depth, bb, ts, w), lambda b, s: (0, b, s, 0))
    else:
        layered = pl.BlockSpec((None, bb, ts, w), lambda b, s: (layer, b, s, 0))
    in_specs = [tok(d), _const_spec(g.shape), _const_spec(w_cols.shape), _const_spec(w_rows_t.shape),
                hist_spec, _const_spec(w_pool.shape), _const_spec(pool_scale.shape)]
    args = [x, g, w_cols, w_rows_t, hist, w_pool, pool_scale]
    aliases = {}
    if kv_all is not None:
        aliases = {len(args): 3, len(args) + 1: 4}
        in_specs += [pl.BlockSpec(memory_space=pl.ANY)] * 2
        args += list(kv_all)
    return pl.pallas_call(
        functools.partial(_in_proj_prompt_kernel, ts=ts, start=0),
        grid=(batch // bb, seq // ts),
        in_specs=in_specs,
        out_specs=[tile_t(w), tile_t(va_rows),
                   pl.BlockSpec((None, ts, w), lambda b, s: (b, s, 0)), layered, layered] + pool_specs,
        out_shape=[tile_t_shape(w), tile_t_shape(va_rows), tok_shape(w, BF16),
                   jax.ShapeDtypeStruct((depth, batch, seq, w), F32),
                   jax.ShapeDtypeStruct((depth, batch, seq, w), F32)] + pool_shapes,
        input_output_aliases=aliases,
        scratch_shapes=[pltpu.VMEM((bb, HIST_ROWS + ts, POOL_WIDTH), F32)],
        compiler_params=_params("arbitrary", "arbitrary"),
        name="in_proj_prompt",
    )(*args)


def _in_proj_sample(x, g, w_cols, hist, w_pool, pool_scale, start):
    batch, seq, d = x.shape
    bb, ts = _row_tiling(batch, seq)
    w = DIFF_WIDTH
    tok, hist_spec, tok_shape, pool_specs, pool_shapes = _in_proj_common(batch, seq, bb, ts)
    return pl.pallas_call(
        functools.partial(_in_proj_sample_kernel, bb=bb, ts=ts, start=start),
        grid=(batch // bb, seq // ts),
        in_specs=[tok(d), _const_spec(g.shape), _const_spec(w_cols.shape),
                  hist_spec, _const_spec(w_pool.shape), _const_spec(pool_scale.shape)],
        out_specs=[tok(w), tok(w), tok(w)] + pool_specs,
        out_shape=[tok_shape(w, BF16), tok_shape(w, F32), tok_shape(w, F32)] + pool_shapes,
        scratch_shapes=[pltpu.VMEM((bb, HIST_ROWS + ts, POOL_WIDTH), F32)],
        compiler_params=_params("arbitrary", "arbitrary"),
        name="in_proj_sample",
    )(x, g, w_cols, hist, w_pool, pool_scale)


def _lambda(lq_ref, lk_ref, lam_init):
    e = jnp.exp(jnp.sum(lq_ref[...] * lk_ref[...], axis=-1, keepdims=True))
    return e[0:1] - e[1:2] + lam_init


def _prompt_attn_kernel(qt_ref, k_ref, vt_ref, lq_ref, lk_ref, sg_ref, o_ref, s_ref, acc_ref, *, t, n_blk, lam_init):
    lam = _lambda(lq_ref, lk_ref, lam_init)

    def masked_q(i, mp):
        zero = jnp.zeros((DIFF_DK, t), BF16)
        if mp == 0:
            return jnp.concatenate([qt_ref[i, :DIFF_DK, :], zero], axis=0)
        return jnp.concatenate([zero, qt_ref[i, DIFF_DK:, :]], axis=0)

    def scores(i, j, slot, mp):
        k_blk = k_ref[pl.ds(pl.multiple_of(j * t, t), t), :]
        s_ref[slot, mp] = jnp.dot(k_blk, masked_q(i, mp), preferred_element_type=F32)

    def update(j, slot, mp, m, masked):
        s = s_ref[slot, mp]
        if masked:
            key = lax.broadcasted_iota(jnp.int32, s.shape, 0)
            qry = lax.broadcasted_iota(jnp.int32, s.shape, 1)
            s = jnp.where(lax.shift_right_logical(key, CHUNK_SHIFT) <= lax.shift_right_logical(qry, CHUNK_SHIFT),
                          s, -jnp.inf)
        m_new = jnp.maximum(m, jnp.max(s, axis=0, keepdims=True))
        alpha = jnp.exp2(m - m_new)
        p = jnp.exp2(s - m_new)
        acc_ref[mp] = alpha * acc_ref[mp] + jnp.dot(vt_ref[j], p.astype(BF16), preferred_element_type=F32)
        return m_new

    def step(i, j, cur, nxt, m0, m1):
        scores(i, j + 1, nxt, 0)
        m0 = update(j, cur, 0, m0, False)
        scores(i, j + 1, nxt, 1)
        m1 = update(j, cur, 1, m1, False)
        return m0, m1

    def finish(i, slot, m0, m1, next_first):
        if next_first is not None:
            scores(i + 1, 0, next_first, 0)
        update(i, slot, 0, m0, True)
        if next_first is not None:
            scores(i + 1, 0, next_first, 1)
        update(i, slot, 1, m1, True)
        normalised = lambda acc: acc[:DIFF_DV] * (1.0 / acc[DIFF_DV:DIFF_DV + 1])
        o = normalised(acc_ref[0]) - lam * normalised(acc_ref[1])
        ms = jnp.mean(o * o, axis=0, keepdims=True)
        y = o * lax.rsqrt(ms + SUBLN_EPS) * sg_ref[...] * (1.0 - lam_init)
        o_ref[pl.ds(pl.multiple_of(i * t, t), t), :] = y.T.astype(BF16)

    def start():
        acc_ref[...] = jnp.zeros(acc_ref.shape, F32)
        return jnp.full((1, t), -jnp.inf, F32)

    def query_block(i, odd, next_first):
        init = start()
        m0, m1 = step(i, 0, 3 if odd else 2, 0, init, init)
        pair = lambda j, c: step(i, j + 1, 1, 0, *step(i, j, 0, 1, *c))
        n_quads = (i - 1) // 4
        m0, m1 = lax.fori_loop(0, n_quads, lambda p, c: pair(4 * p + 3, pair(4 * p + 1, c)), (m0, m1))
        m0, m1 = lax.cond((i - 1) % 4 >= 2, lambda c: pair(4 * n_quads + 1, c), lambda c: c, (m0, m1))
        if odd:
            finish(i, 0, m0, m1, next_first)
        else:
            finish(i, 1, *step(i, i - 1, 0, 1, m0, m1), next_first)

    scores(0, 0, 2, 0)
    scores(0, 0, 2, 1)
    init = start()
    finish(0, 2, init, init, 3 if n_blk > 1 else None)
    if n_blk > 1:
        def two_query_blocks(ip, carry):
            query_block(2 * ip + 1, True, 2)
            query_block(2 * ip + 2, False, 3)
            return carry

        lax.fori_loop(0, (n_blk - 2) // 2, two_query_blocks, 0)
        query_block(n_blk - 1, True, None)


def _prompt_attn(qt, k, vta, lam_q, lam_k, subln_g_col, lam_init):
    batch, n_blk, _, t = qt.shape
    seq = n_blk * t
    assert t % CHUNK == 0 and (t & (t - 1)) == 0
    assert n_blk == 1 or n_blk % 2 == 0
    return pl.pallas_call(
        functools.partial(_prompt_attn_kernel, t=t, n_blk=n_blk, lam_init=lam_init),
        grid=(batch, N_DIFF_HEADS),
        in_specs=[pl.BlockSpec((None, n_blk, DIFF_DV, t), lambda b, h: (b, 0, h, 0)),
                  pl.BlockSpec((None, seq, DIFF_DV), lambda b, h: (b, 0, h)),
                  pl.BlockSpec((None, n_blk, V_AUG_ROWS, t), lambda b, h: (b, 0, h, 0)),
                  _const_spec(lam_q.shape), _const_spec(lam_k.shape), _const_spec(subln_g_col.shape)],
        out_specs=pl.BlockSpec((None, seq, DIFF_DV), lambda b, h: (b, 0, h)),
        out_shape=jax.ShapeDtypeStruct((batch, seq, DIFF_WIDTH), BF16),
        scratch_shapes=[pltpu.VMEM((4, 2, t, t), F32),
                        pltpu.VMEM((2, V_AUG_ROWS, t), F32)],
        compiler_params=_params("arbitrary", "arbitrary"),
        name="prompt_attn",
    )(qt, k, vta, lam_q, lam_k, subln_g_col)


def _sample_attn_kernel(q_ref, kn_ref, vn_ref, ck_ref, cv_ref, lq_ref, lk_ref, sg_ref, o_ref,
                        *, past, n_new, lam_init):
    lam = _lambda(lq_ref, lk_ref, lam_init)
    sg = sg_ref[...]
    lane = lax.broadcasted_iota(jnp.int32, (n_new, DIFF_DV), 1)
    r = lax.broadcasted_iota(jnp.int32, (n_new, n_new), 0)
    c = lax.broadcasted_iota(jnp.int32, (n_new, n_new), 1)
    visible = lax.shift_right_logical(past + c, CHUNK_SHIFT) <= lax.shift_right_logical(past + r, CHUNK_SHIFT)
    visible2 = jnp.concatenate([visible, visible], axis=0)
    head_cols = lambda h: slice(h * DIFF_DV, (h + 1) * DIFF_DV)

    def masked_q(h, mp):
        q_blk = q_ref[:, head_cols(h)].astype(F32)
        qm = q_blk if mp == h % 2 else pltpu.roll(q_blk, DIFF_DK, axis=1)
        return jnp.where((lane >= DIFF_DK) == bool(h % 2), qm, 0.0).astype(BF16)

    v_c = [cv_ref[:, head_cols(h)].astype(BF16) for h in range(N_DIFF_HEADS)]
    v_n = [vn_ref[:, head_cols(h)].astype(BF16) for h in range(N_DIFF_HEADS)]
    o_maps = {}
    for mp in range(2):
        for pair in range(N_DIFF_HEADS // 2):
            heads = (2 * pair, 2 * pair + 1)
            ks = head_cols(2 * mp + pair)
            qq = jnp.concatenate([masked_q(h, mp) for h in heads], axis=0)
            s_c = lax.dot_general(qq, ck_ref[:, ks].astype(BF16), _NT, preferred_element_type=F32)
            s_n = lax.dot_general(qq, kn_ref[:, ks].astype(BF16), _NT, preferred_element_type=F32)
            s_n = jnp.where(visible2, s_n, -jnp.inf)
            mx = jnp.maximum(jnp.max(s_c, axis=-1, keepdims=True), jnp.max(s_n, axis=-1, keepdims=True))
            p_c = jnp.exp2(s_c - mx)
            p_n = jnp.exp2(s_n - mx)
            inv_l = 1.0 / (jnp.sum(p_c, axis=-1, keepdims=True) + jnp.sum(p_n, axis=-1, keepdims=True))
            p_c, p_n = p_c.astype(BF16), p_n.astype(BF16)
            for n, h in enumerate(heads):
                rows = slice(n * n_new, (n + 1) * n_new)
                o = jnp.dot(p_c[rows], v_c[h], preferred_element_type=F32) + \
                    jnp.dot(p_n[rows], v_n[h], preferred_element_type=F32)
                o_maps[h, mp] = o * inv_l[rows]
    for h in range(N_DIFF_HEADS):
        o = o_maps[h, 0] - lam * o_maps[h, 1]
        ms = jnp.mean(o * o, axis=-1, keepdims=True)
        o_ref[:, head_cols(h)] = (o * lax.rsqrt(ms + SUBLN_EPS) * sg * (1.0 - lam_init)).astype(BF16)


def _sample_attn(q, k_new, v_new, cache_k, cache_v, layer, lam_q, lam_k, subln_g, lam_init):
    batch, n_new, _ = q.shape
    past = cache_k.shape[2]
    w = DIFF_WIDTH
    new = pl.BlockSpec((None, n_new, w), lambda b: (b, 0, 0))
    cached = pl.BlockSpec((None, None, past, w), lambda b: (layer, b, 0, 0))
    return pl.pallas_call(
        functools.partial(_sample_attn_kernel, past=past, n_new=n_new, lam_init=lam_init),
        grid=(batch,),
        in_specs=[new, new, new, cached, cached,
                  _const_spec(lam_q.shape), _const_spec(lam_k.shape), _const_spec(subln_g.shape)],
        out_specs=new,
        out_shape=jax.ShapeDtypeStruct((batch, n_new, w), BF16),
        compiler_params=_params("arbitrary"),
        name="sample_attn",
    )(q, k_new, v_new, cache_k, cache_v, lam_q, lam_k, subln_g)


def _mem_kv_kernel(m_ref, g_ref, w_ref, k_ref, v_ref, kb_ref, vb_ref):
    d = m_ref.shape[-1]
    z = jnp.dot(_rms_bf16(m_ref[...], g_ref[...], NORM_EPS), w_ref[...], preferred_element_type=F32)
    k_ref[...] = z[:, :d]
    v_ref[...] = z[:, d:]
    kb_ref[...] = z[:, :d].astype(BF16)
    vb_ref[...] = z[:, d:].astype(BF16)


def _mem_kv(mem, g, w_kv):
    batch, n_mem, d = mem.shape
    rows = batch * n_mem
    tm = min(ROW_TILE, rows)
    assert rows % tm == 0
    tok = pl.BlockSpec((tm, d), lambda i: (i, 0))
    outs = pl.pallas_call(
        _mem_kv_kernel,
        grid=(rows // tm,),
        in_specs=[tok, _const_spec(g.shape), _const_spec(w_kv.shape)],
        out_specs=[tok] * 4,
        out_shape=[jax.ShapeDtypeStruct((rows, d), F32)] * 2 + [jax.ShapeDtypeStruct((rows, d), BF16)] * 2,
        compiler_params=_params("arbitrary"),
        name="mem_kv",
    )(mem.reshape(rows, d), g, w_kv)
    return [o.reshape(batch, n_mem, d) for o in outs]


def _mix_kernel(x_ref, a_ref, p_ref, wo_ref, g_ref, wq_ref, mk_ref, mv_ref, wox_ref, o_ref, *, bb, ts):
    d = x_ref.shape[-1]
    m = bb * ts
    hd = d // N_X_HEADS
    ap = jnp.concatenate([a_ref[...].reshape(m, -1), p_ref[...].reshape(m, -1)], axis=-1)
    x1 = x_ref[...].reshape(m, d) + jnp.dot(ap, wo_ref[...], preferred_element_type=F32)
    h = _rms_bf16(x1, g_ref[...], NORM_EPS)
    q = jnp.dot(h, wq_ref[...], preferred_element_type=F32).astype(BF16)
    units = [(i, hh) for i in range(bb) for hh in range(N_X_HEADS)]
    scores = lambda i, hh: lax.dot_general(q[i * ts:(i + 1) * ts, hh * hd:(hh + 1) * hd],
                                           mk_ref[i, :, hh * hd:(hh + 1) * hd], _NT, preferred_element_type=F32)
    outs = []
    s_next = scores(*units[0])
    for n, (i, hh) in enumerate(units):
        s = s_next
        if n + 1 < len(units):
            s_next = scores(*units[n + 1])
        p = jnp.exp(s - jnp.max(s, axis=-1, keepdims=True))
        inv_l = 1.0 / jnp.sum(p, axis=-1, keepdims=True)
        o = jnp.dot(p.astype(BF16), mv_ref[i, :, hh * hd:(hh + 1) * hd], preferred_element_type=F32) * inv_l
        outs.append(o.astype(BF16))
    rows = [jnp.concatenate(outs[i * N_X_HEADS:(i + 1) * N_X_HEADS], axis=-1) for i in range(bb)]
    o_all = rows[0] if bb == 1 else jnp.concatenate(rows, axis=0)
    o_ref[...] = (x1 + jnp.dot(o_all, wox_ref[...], preferred_element_type=F32)).reshape(bb, ts, d)


def _mix(x, a, pool_y, w_out, g, wq, mk, mv, wo_x):
    batch, seq, d = x.shape
    bb, ts = _row_tiling(batch, seq, WIDE_ROW_TILE if seq >= WIDE_ROW_TILE else None)
    n_mem = mk.shape[1]
    tok = lambda width: pl.BlockSpec((bb, ts, width), lambda b, s: (b, s, 0))
    mem = pl.BlockSpec((bb, n_mem, d), lambda b, s: (b, 0, 0))
    return pl.pallas_call(
        functools.partial(_mix_kernel, bb=bb, ts=ts),
        grid=(batch // bb, seq // ts),
        in_specs=[tok(d), tok(a.shape[-1]), tok(pool_y.shape[-1]), _const_spec(w_out.shape), _const_spec(g.shape),
                  _const_spec(wq.shape), mem, mem, _const_spec(wo_x.shape)],
        out_specs=tok(d),
        out_shape=jax.ShapeDtypeStruct(x.shape, F32),
        compiler_params=_params("arbitrary", "arbitrary"),
        name="mix",
    )(x, a, pool_y, w_out, g, wq, mk, mv, wo_x)


def _mlp_kernel(x_ref, g_ref, wu_ref, wd_ref, fg_ref, o_ref, *, bb, ts, final):
    d = x_ref.shape[-1]
    m = bb * ts
    x = x_ref[...].reshape(m, d)
    h = _rms_bf16(x, g_ref[...], NORM_EPS)
    ck = wu_ref.shape[1] // MLP_CHUNKS
    acc = x
    for j in range(MLP_CHUNKS):
        up = jnp.maximum(jnp.dot(h, wu_ref[:, j * ck:(j + 1) * ck], preferred_element_type=F32), 0.0)
        acc = acc + jnp.dot((up * up).astype(BF16), wd_ref[j * ck:(j + 1) * ck, :], preferred_element_type=F32)
    if final:
        ms = jnp.mean(acc * acc, axis=-1, keepdims=True)
        acc = acc * lax.rsqrt(ms + NORM_EPS) * fg_ref[...]
    o_ref[...] = acc.reshape(bb, ts, d)


def _mlp(x, g, w_up, w_down, final_g, final):
    batch, seq, d = x.shape
    bb, ts = _row_tiling(batch, seq, WIDE_ROW_TILE if seq >= WIDE_ROW_TILE else None)
    assert w_up.shape[1] % MLP_CHUNKS == 0
    tok = pl.BlockSpec((bb, ts, d), lambda b, s: (b, s, 0))
    return pl.pallas_call(
        functools.partial(_mlp_kernel, bb=bb, ts=ts, final=final),
        grid=(batch // bb, seq // ts),
        in_specs=[tok, _const_spec(g.shape), _const_spec(w_up.shape), _const_spec(w_down.shape),
                  _const_spec(final_g.shape)],
        out_specs=tok,
        out_shape=jax.ShapeDtypeStruct(x.shape, F32),
        compiler_params=_params("arbitrary", "arbitrary"),
        name="mlp",
    )(x, g, w_up, w_down, final_g)


def _head_major(w):
    d = w.shape[0]
    return w.reshape(d, 2, N_DIFF_HEADS, DIFF_DK).transpose(0, 2, 1, 3).reshape(d, DIFF_WIDTH)


def _in_weights(w_in):
    w = DIFF_WIDTH
    wq, wk, wv, wu = w_in[:, :w], w_in[:, w:2 * w], w_in[:, 2 * w:3 * w], w_in[:, 3 * w:]
    wq = _head_major(wq) * (DIFF_DK ** -0.5 * LOG2E)
    cols_p = jnp.concatenate([wv, wk, wu], axis=1).astype(BF16)
    rows_t = wq.T.astype(BF16)
    cols_s = jnp.concatenate([wq, wv, wk, wu], axis=1).astype(BF16)
    return cols_p, rows_t, cols_s


def kernel(x_prompt, x_sample, cache_k, cache_v, state_pool, cache_mem_k, cache_mem_v, mem_prompt,
           norm_mix_g, w_in, lam_q, lam_k, subln_g, w_pool, pool_scale, w_out,
           norm_x_g, norm_mem_g, wq_x, wk_x, wv_x, wo_x, norm_mlp_g, w_up, w_down, final_g):
    depth = w_in.shape[0]
    batch, seq, d = x_prompt.shape
    dec_batch, dec_seq, _ = x_sample.shape
    past = cache_k.shape[2]
    n_mem = mem_prompt.shape[1]
    hd = d // N_X_HEADS

    ck = cache_k.reshape(depth, dec_batch, past, DIFF_WIDTH)
    cv = cache_v.reshape(depth, dec_batch, past, DIFF_WIDTH)
    cmk = cache_mem_k.reshape(depth, dec_batch, n_mem, d).astype(BF16)
    cmv = cache_mem_v.reshape(depth, dec_batch, n_mem, d).astype(BF16)
    hist_s = jnp.pad(state_pool, ((0, 0), (0, 0), (HIST_ROWS - POOL_HIST, 0), (0, 0)))
    hist_p = jnp.zeros((batch, HIST_ROWS, POOL_WIDTH), F32)
    row = lambda v: v.reshape(1, -1)
    fg = row(final_g)

    xp, xs = x_prompt, x_sample
    outs = [[] for _ in range(6)]
    kv_prompt = None
    for l in range(depth):
        lam_init = 0.8 - 0.6 * math.exp(-0.3 * l)
        last = l == depth - 1
        w_cols_p, w_rows_t, w_cols_s = _in_weights(w_in[l])
        wp = w_pool[l].astype(BF16)
        ps = row(pool_scale[l])
        g_mix, g_x, g_mlp, sg = row(norm_mix_g[l]), row(norm_x_g[l]), row(norm_mlp_g[l]), row(subln_g[l])
        w_o = w_out[l].astype(BF16)
        wq = (wq_x[l] * hd ** -0.5).astype(BF16)
        wo = wo_x[l].astype(BF16)
        wu, wd = w_up[l].astype(BF16), w_down[l].astype(BF16)
        w_kv = jnp.concatenate([wk_x[l], wv_x[l]], axis=1).astype(BF16)

        qt, vta, ka, *kv_prompt, py, ph = _in_proj_prompt(xp, g_mix, w_cols_p, w_rows_t, hist_p, wp, ps, l, depth,
                                                          kv_prompt)
        a = _prompt_attn(qt, ka, vta, lam_q[l], lam_k[l], sg.reshape(-1, 1), lam_init)
        mk, mv, mkb, mvb = _mem_kv(mem_prompt, row(norm_mem_g[l]), w_kv)
        xp = _mix(xp, a, py, w_o, g_x, wq, mkb, mvb, wo)
        xp = _mlp(xp, g_mlp, wu, wd, fg, last)
        for dst, val in zip(outs[:3], (ph[:, HIST_ROWS - POOL_HIST:], mk, mv)):
            dst.append(val)

        q, ko, vo, py, ph = _in_proj_sample(xs, g_mix, w_cols_s, hist_s[l], wp, ps, past)
        a = _sample_attn(q, ko, vo, ck, cv, l, lam_q[l], lam_k[l], sg, lam_init)
        xs = _mix(xs, a, py, w_o, g_x, wq, cmk[l], cmv[l], wo)
        xs = _mlp(xs, g_mlp, wu, wd, fg, last)
        for dst, val in zip(outs[3:], (ko, vo, ph[:, HIST_ROWS - POOL_HIST:])):
            dst.append(val)

    kp, vp = kv_prompt
    pp, mkp, mvp, ks, vs, pps = [jnp.stack(o) for o in outs]
    return (xp, xs,
            kp.reshape(depth, batch, seq, 2, N_DIFF_HEADS, DIFF_DK),
            vp.reshape(depth, batch, seq, N_DIFF_HEADS, DIFF_DV),
            pp,
            mkp.reshape(depth, batch, n_mem, N_X_HEADS, hd),
            mvp.reshape(depth, batch, n_mem, N_X_HEADS, hd),
            ks.reshape(depth, dec_batch, dec_seq, 2, N_DIFF_HEADS, DIFF_DK),
            vs.reshape(depth, dec_batch, dec_seq, N_DIFF_HEADS, DIFF_DV),
            pps)
```

```python
import functools
import math

import jax
import jax.numpy as jnp
from jax import lax
from jax.experimental import pallas as pl
from jax.experimental.pallas import tpu as pltpu

F32 = jnp.float32
BF16 = jnp.bfloat16

CHUNK = 64
CHUNK_SHIFT = CHUNK.bit_length() - 1
N_DIFF_HEADS = 4
DIFF_DK = 64
DIFF_DV = 2 * DIFF_DK
DIFF_WIDTH = N_DIFF_HEADS * DIFF_DV
V_AUG_ROWS = DIFF_DV + 16
POOL_WINDOWS = (2, 4, 8, 16)
POOL_GC = 128
POOL_WIDTH = len(POOL_WINDOWS) * POOL_GC
assert all(w & (w - 1) == 0 for w in POOL_WINDOWS)
POOL_HIST = max(POOL_WINDOWS) - 1
HIST_ROWS = 16
N_X_HEADS = 4
NORM_EPS = 1e-6
SUBLN_EPS = 1e-5
LOG2E = 1.4426950408889634

ROW_TILE = 512
WIDE_ROW_TILE = 1024
MLP_CHUNKS = 4
V7X_VMEM_LIMIT = 56 * 2**20

_NT = (((1,), (1,)), ((), ()))


def _rms_bf16(x, g, eps):
    ms = jnp.mean(x * x, axis=-1, keepdims=True)
    return (x * lax.rsqrt(ms + eps) * g).astype(BF16)


def _row_tiling(batch, seq, rows=None):
    rows = ROW_TILE if rows is None else rows
    if seq >= rows:
        assert seq % rows == 0
        return 1, rows
    bb = max(1, min(batch, rows // seq))
    assert batch % bb == 0 and seq % 8 == 0
    return bb, seq


def _params(*semantics):
    return pltpu.CompilerParams(dimension_semantics=semantics, vmem_limit_bytes=V7X_VMEM_LIMIT)


def _const_spec(shape):
    nd = len(shape)
    return pl.BlockSpec(shape, lambda *_: (0,) * nd, pipeline_mode=pl.Buffered(1))


def _project(x_ref, g_ref, w_ref, ko_ref, vo_ref, ext_ref, hist_ref, ph_ref, wp_ref, ps_ref, py_ref, bb, ts, start):
    d = x_ref.shape[-1]
    m = bb * ts
    w = DIFF_WIDTH
    h = _rms_bf16(x_ref[...].reshape(m, d), g_ref[...], NORM_EPS)
    u = jnp.dot(h, w_ref[:, w_ref.shape[1] - POOL_WIDTH:], preferred_element_type=F32)

    @pl.when(pl.program_id(1) == 0)
    def _():
        ext_ref[:, 0:HIST_ROWS, :] = hist_ref[...]

    ext_ref[:, HIST_ROWS:HIST_ROWS + ts, :] = u.reshape(bb, ts, POOL_WIDTH)
    groups = _pool_windows(ext_ref, ph_ref, bb, ts, start)
    half = len(groups) // 2
    n_lead = w_ref.shape[1] - 2 * w - POOL_WIDTH
    first = jnp.dot(h, w_ref[:, :n_lead], preferred_element_type=F32) if n_lead else None
    _pool_project(groups[:half], 0, wp_ref, ps_ref, py_ref, bb, ts)
    v = jnp.dot(h, w_ref[:, n_lead:n_lead + w], preferred_element_type=F32)
    vo_ref[...] = v.reshape(bb, ts, w)
    _pool_project(groups[half:], half, wp_ref, ps_ref, py_ref, bb, ts)
    k = jnp.dot(h, w_ref[:, n_lead + w:n_lead + 2 * w], preferred_element_type=F32)
    ko_ref[...] = k.reshape(bb, ts, w)
    return h, first, v, k


def _pool_windows(ext_ref, ph_ref, bb, ts, start):
    pos = start + pl.program_id(1) * ts + lax.broadcasted_iota(jnp.int32, (ts, 1), 0)
    groups = []
    for gi, win_len in enumerate(POOL_WINDOWS):
        cols = slice(gi * POOL_GC, (gi + 1) * POOL_GC)
        inv_cnt = 1.0 / jnp.minimum(win_len, pos + 1).astype(F32)
        pooled = []
        for i in range(bb):
            win = ext_ref[i, :, cols]
            k = 1
            while k < win_len:
                win = win + pltpu.roll(win, k, axis=0)
                k *= 2
            u = ext_ref[i, HIST_ROWS:HIST_ROWS + ts, cols]
            pooled.append((win[HIST_ROWS:] * inv_cnt - u).astype(BF16))
        groups.append(pooled[0] if bb == 1 else jnp.concatenate(pooled, axis=0))
    tail = ext_ref[:, ts:ts + HIST_ROWS, :]
    ph_ref[...] = tail
    ext_ref[:, 0:HIST_ROWS, :] = tail
    return groups


def _pool_project(groups, first_group, wp_ref, ps_ref, py_ref, bb, ts):
    for gi, pooled in enumerate(groups, first_group):
        cols = slice(gi * POOL_GC, (gi + 1) * POOL_GC)
        y = jnp.dot(pooled, wp_ref[gi], preferred_element_type=F32) * ps_ref[:, cols]
        py_ref[:, :, cols] = y.astype(BF16).reshape(bb, ts, POOL_GC)


def _in_proj_prompt_kernel(x_ref, g_ref, w_ref, wt_ref, hist_ref, wp_ref, ps_ref, *refs, ts, start):
    qt_ref, vta_ref, ka_ref, ko_ref, vo_ref, py_ref, ph_ref, ext_ref = refs[-8:]
    if len(refs) == 8:
        for ref in (ko_ref, vo_ref):
            if ref.shape[0] > 1:
                ref[1:] = jnp.zeros((ref.shape[0] - 1,) + ref.shape[1:], F32)
        ko_ref, vo_ref = ko_ref.at[0], vo_ref.at[0]
    h, _, v, k = _project(x_ref, g_ref, w_ref, ko_ref, vo_ref, ext_ref, hist_ref, ph_ref, wp_ref, ps_ref, py_ref,
                       1, ts, start)
    blocks = [k[:, j * DIFF_DV:(j + 1) * DIFF_DV] for j in range(DIFF_WIDTH // DIFF_DV)]
    low_half = lax.broadcasted_iota(jnp.int32, (ts, DIFF_DV), 1) < DIFF_DK
    for hd in range(N_DIFF_HEADS):
        map0, map1 = blocks[hd // 2], blocks[N_DIFF_HEADS // 2 + hd // 2]
        if hd % 2 == 0:
            k_hm = jnp.where(low_half, map0, pltpu.roll(map1, DIFF_DK, axis=1))
        else:
            k_hm = jnp.where(low_half, pltpu.roll(map0, DIFF_DK, axis=1), map1)
        ka_ref[:, hd * DIFF_DV:(hd + 1) * DIFF_DV] = k_hm.astype(BF16)
    qt_ref[...] = lax.dot_general(wt_ref[...], h, _NT, preferred_element_type=F32).astype(BF16)
    ones = (lax.broadcasted_iota(jnp.int32, (V_AUG_ROWS - DIFF_DV, ts), 0) == 0).astype(BF16)
    for hd in range(N_DIFF_HEADS):
        vta_ref[hd * V_AUG_ROWS:hd * V_AUG_ROWS + DIFF_DV, :] = v[:, hd * DIFF_DV:(hd + 1) * DIFF_DV].T.astype(BF16)
        vta_ref[hd * V_AUG_ROWS + DIFF_DV:(hd + 1) * V_AUG_ROWS, :] = ones


def _in_proj_sample_kernel(x_ref, g_ref, w_ref, hist_ref, wp_ref, ps_ref,
                           q_ref, ko_ref, vo_ref, py_ref, ph_ref, ext_ref, *, bb, ts, start):
    _, q, _, _ = _project(x_ref, g_ref, w_ref, ko_ref, vo_ref, ext_ref, hist_ref, ph_ref, wp_ref, ps_ref, py_ref,
                       bb, ts, start)
    q_ref[...] = q.astype(BF16).reshape(bb, ts, DIFF_WIDTH)


def _in_proj_common(batch, seq, bb, ts):
    assert seq >= HIST_ROWS
    tok = lambda width: pl.BlockSpec((bb, ts, width), lambda b, s: (b, s, 0))
    hist_spec = pl.BlockSpec((bb, HIST_ROWS, POOL_WIDTH), lambda b, s: (b, 0, 0))
    tok_shape = lambda width, dt: jax.ShapeDtypeStruct((batch, seq, width), dt)
    pool_specs = [tok(POOL_WIDTH), hist_spec]
    pool_shapes = [tok_shape(POOL_WIDTH, BF16), jax.ShapeDtypeStruct((batch, HIST_ROWS, POOL_WIDTH), F32)]
    return tok, hist_spec, tok_shape, pool_specs, pool_shapes


def _in_proj_prompt(x, g, w_cols, w_rows_t, hist, w_pool, pool_scale, layer, depth, kv_all):
    batch, seq, d = x.shape
    bb, ts = _row_tiling(batch, seq)
    assert bb == 1
    w = DIFF_WIDTH
    tok, hist_spec, tok_shape, pool_specs, pool_shapes = _in_proj_common(batch, seq, bb, ts)
    va_rows = N_DIFF_HEADS * V_AUG_ROWS
    tile_t = lambda rows: pl.BlockSpec((None, None, rows, ts), lambda b, s: (b, s, 0, 0))
    tile_t_shape = lambda rows: jax.ShapeDtypeStruct((batch, seq // ts, rows, ts), BF16)
    if kv_all is None:
        assert layer == 0
        layered = pl.BlockSpec((depth, bb, ts, w), lambda b, s: (0, b, s, 0))
    else:
        layered = pl.BlockSpec((None, bb, ts, w), lambda b, s: (layer, b, s, 0))
    in_specs = [tok(d), _const_spec(g.shape), _const_spec(w_cols.shape), _const_spec(w_rows_t.shape),
                hist_spec, _const_spec(w_pool.shape), _const_spec(pool_scale.shape)]
    args = [x, g, w_cols, w_rows_t, hist, w_pool, pool_scale]
    aliases = {}
    if kv_all is not None:
        aliases = {len(args): 3, len(args) + 1: 4}
        in_specs += [pl.BlockSpec(memory_space=pl.ANY)] * 2
        args += list(kv_all)
    return pl.pallas_call(
        functools.partial(_in_proj_prompt_kernel, ts=ts, start=0),
        grid=(batch // bb, seq // ts),
        in_specs=in_specs,
        out_specs=[tile_t(w), tile_t(va_rows),
                   pl.BlockSpec((None, ts, w), lambda b, s: (b, s, 0)), layered, layered] + pool_specs,
        out_shape=[tile_t_shape(w), tile_t_shape(va_rows), tok_shape(w, BF16),
                   jax.ShapeDtypeStruct((depth, batch, seq, w), F32),
                   jax.ShapeDtypeStruct((depth, batch, seq, w), F32)] + pool_shapes,
        input_output_aliases=aliases,
        scratch_shapes=[pltpu.VMEM((bb, HIST_ROWS + ts, POOL_WIDTH), F32)],
        compiler_params=_params("arbitrary", "arbitrary"),
        name="in_proj_prompt",
    )(*args)


def _in_proj_sample(x, g, w_cols, hist, w_pool, pool_scale, start):
    batch, seq, d = x.shape
    bb, ts = _row_tiling(batch, seq)
    w = DIFF_WIDTH
    tok, hist_spec, tok_shape, pool_specs, pool_shapes = _in_proj_common(batch, seq, bb, ts)
    return pl.pallas_call(
        functools.partial(_in_proj_sample_kernel, bb=bb, ts=ts, start=start),
        grid=(batch // bb, seq // ts),
        in_specs=[tok(d), _const_spec(g.shape), _const_spec(w_cols.shape),
                  hist_spec, _const_spec(w_pool.shape), _const_spec(pool_scale.shape)],
        out_specs=[tok(w), tok(w), tok(w)] + pool_specs,
        out_shape=[tok_shape(w, BF16), tok_shape(w, F32), tok_shape(w, F32)] + pool_shapes,
        scratch_shapes=[pltpu.VMEM((bb, HIST_ROWS + ts, POOL_WIDTH), F32)],
        compiler_params=_params("arbitrary", "arbitrary"),
        name="in_proj_sample",
    )(x, g, w_cols, hist, w_pool, pool_scale)


def _lambda(lq_ref, lk_ref, lam_init):
    e = jnp.exp(jnp.sum(lq_ref[...] * lk_ref[...], axis=-1, keepdims=True))
    return e[0:1] - e[1:2] + lam_init


def _prompt_attn_kernel(qt_ref, k_ref, vt_ref, lq_ref, lk_ref, sg_ref, o_ref, s_ref, acc_ref, *, t, n_blk, lam_init):
    lam = _lambda(lq_ref, lk_ref, lam_init)

    def masked_q(i, mp):
        zero = jnp.zeros((DIFF_DK, t), BF16)
        if mp == 0:
            return jnp.concatenate([qt_ref[i, :DIFF_DK, :], zero], axis=0)
        return jnp.concatenate([zero, qt_ref[i, DIFF_DK:, :]], axis=0)

    def scores(i, j, slot, mp):
        k_blk = k_ref[pl.ds(pl.multiple_of(j * t, t), t), :]
        s_ref[slot, mp] = jnp.dot(k_blk, masked_q(i, mp), preferred_element_type=F32)

    def update(j, slot, mp, m, masked):
        s = s_ref[slot, mp]
        if masked:
            key = lax.broadcasted_iota(jnp.int32, s.shape, 0)
            qry = lax.broadcasted_iota(jnp.int32, s.shape, 1)
            s = jnp.where(lax.shift_right_logical(key, CHUNK_SHIFT) <= lax.shift_right_logical(qry, CHUNK_SHIFT),
                          s, -jnp.inf)
        m_new = jnp.maximum(m, jnp.max(s, axis=0, keepdims=True))
        alpha = jnp.exp2(m - m_new)
        p = jnp.exp2(s - m_new)
        acc_ref[mp] = alpha * acc_ref[mp] + jnp.dot(vt_ref[j], p.astype(BF16), preferred_element_type=F32)
        return m_new

    def chunk_masked(s, key0, qry0):
        key = lax.broadcasted_iota(jnp.int32, s.shape, 0) + key0
        qry = lax.broadcasted_iota(jnp.int32, s.shape, 1) + qry0
        return jnp.where(lax.shift_right_logical(key, CHUNK_SHIFT) <= lax.shift_right_logical(qry, CHUNK_SHIFT),
                         s, -jnp.inf)

    def update_diagonal(i, slot, mp, m):
        hf = t // 2
        if hf % DIFF_DV or hf % CHUNK:
            return update(i, slot, mp, m, True)
        s_top = chunk_masked(s_ref[slot, mp, :hf, :], 0, 0)
        s_bot = chunk_masked(s_ref[slot, mp, hf:, hf:], hf, hf)
        m_top = jnp.max(s_top, axis=0, keepdims=True)
        m_blk = jnp.concatenate([m_top[:, :hf], jnp.maximum(m_top[:, hf:], jnp.max(s_bot, axis=0, keepdims=True))],
                                axis=1)
        m_new = jnp.maximum(m, m_blk)
        alpha = jnp.exp2(m - m_new)
        p_top = jnp.exp2(s_top - m_new).astype(BF16)
        p_bot = jnp.exp2(s_bot - m_new[:, hf:]).astype(BF16)
        acc_ref[mp] = alpha * acc_ref[mp] + jnp.dot(vt_ref[i, :, :hf], p_top, preferred_element_type=F32)
        acc_ref[mp, :, hf:] = acc_ref[mp, :, hf:] + jnp.dot(vt_ref[i, :, hf:], p_bot, preferred_element_type=F32)
        return m_new

    def step(i, j, cur, nxt, m0, m1):
        scores(i, j + 1, nxt, 0)
        m0 = update(j, cur, 0, m0, False)
        scores(i, j + 1, nxt, 1)
        m1 = update(j, cur, 1, m1, False)
        return m0, m1

    def finish(i, slot, m0, m1, next_first):
        if next_first is not None:
            scores(i + 1, 0, next_first, 0)
        update_diagonal(i, slot, 0, m0)
        if next_first is not None:
            scores(i + 1, 0, next_first, 1)
        update_diagonal(i, slot, 1, m1)
        normalised = lambda acc: acc[:DIFF_DV] * (1.0 / acc[DIFF_DV:DIFF_DV + 1])
        o = normalised(acc_ref[0]) - lam * normalised(acc_ref[1])
        ms = jnp.mean(o * o, axis=0, keepdims=True)
        y = o * lax.rsqrt(ms + SUBLN_EPS) * sg_ref[...] * (1.0 - lam_init)
        o_ref[pl.ds(pl.multiple_of(i * t, t), t), :] = y.T.astype(BF16)

    def start():
        acc_ref[...] = jnp.zeros(acc_ref.shape, F32)
        return jnp.full((1, t), -jnp.inf, F32)

    def query_block(i, odd, next_first):
        init = start()
        m0, m1 = step(i, 0, 3 if odd else 2, 0, init, init)
        pair = lambda j, c: step(i, j + 1, 1, 0, *step(i, j, 0, 1, *c))
        n_quads = (i - 1) // 4
        m0, m1 = lax.fori_loop(0, n_quads, lambda p, c: pair(4 * p + 3, pair(4 * p + 1, c)), (m0, m1))
        m0, m1 = lax.cond((i - 1) % 4 >= 2, lambda c: pair(4 * n_quads + 1, c), lambda c: c, (m0, m1))
        if odd:
            finish(i, 0, m0, m1, next_first)
        else:
            finish(i, 1, *step(i, i - 1, 0, 1, m0, m1), next_first)

    scores(0, 0, 2, 0)
    scores(0, 0, 2, 1)
    init = start()
    finish(0, 2, init, init, 3 if n_blk > 1 else None)
    if n_blk > 1:
        def two_query_blocks(ip, carry):
            query_block(2 * ip + 1, True, 2)
            query_block(2 * ip + 2, False, 3)
            return carry

        lax.fori_loop(0, (n_blk - 2) // 2, two_query_blocks, 0)
        query_block(n_blk - 1, True, None)


def _prompt_attn(qt, k, vta, lam_q, lam_k, subln_g_col, lam_init):
    batch, n_blk, _, t = qt.shape
    seq = n_blk * t
    assert t % CHUNK == 0 and (t & (t - 1)) == 0
    assert n_blk == 1 or n_blk % 2 == 0
    return pl.pallas_call(
        functools.partial(_prompt_attn_kernel, t=t, n_blk=n_blk, lam_init=lam_init),
        grid=(batch, N_DIFF_HEADS),
        in_specs=[pl.BlockSpec((None, n_blk, DIFF_DV, t), lambda b, h: (b, 0, h, 0)),
                  pl.BlockSpec((None, seq, DIFF_DV), lambda b, h: (b, 0, h)),
                  pl.BlockSpec((None, n_blk, V_AUG_ROWS, t), lambda b, h: (b, 0, h, 0)),
                  _const_spec(lam_q.shape), _const_spec(lam_k.shape), _const_spec(subln_g_col.shape)],
        out_specs=pl.BlockSpec((None, seq, DIFF_DV), lambda b, h: (b, 0, h)),
        out_shape=jax.ShapeDtypeStruct((batch, seq, DIFF_WIDTH), BF16),
        scratch_shapes=[pltpu.VMEM((4, 2, t, t), F32),
                        pltpu.VMEM((2, V_AUG_ROWS, t), F32)],
        compiler_params=_params("arbitrary", "arbitrary"),
        name="prompt_attn",
    )(qt, k, vta, lam_q, lam_k, subln_g_col)


def _sample_attn_kernel(q_ref, kn_ref, vn_ref, ck_ref, cv_ref, lq_ref, lk_ref, sg_ref, o_ref,
                        *, past, n_new, lam_init):
    lam = _lambda(lq_ref, lk_ref, lam_init)
    sg = sg_ref[...]
    lane = lax.broadcasted_iota(jnp.int32, (n_new, DIFF_DV), 1)
    r = lax.broadcasted_iota(jnp.int32, (n_new, n_new), 0)
    c = lax.broadcasted_iota(jnp.int32, (n_new, n_new), 1)
    visible = lax.shift_right_logical(past + c, CHUNK_SHIFT) <= lax.shift_right_logical(past + r, CHUNK_SHIFT)
    visible2 = jnp.concatenate([visible, visible], axis=0)
    head_cols = lambda h: slice(h * DIFF_DV, (h + 1) * DIFF_DV)

    def masked_q(h, mp):
        q_blk = q_ref[:, head_cols(h)].astype(F32)
        qm = q_blk if mp == h % 2 else pltpu.roll(q_blk, DIFF_DK, axis=1)
        return jnp.where((lane >= DIFF_DK) == bool(h % 2), qm, 0.0).astype(BF16)

    v_c = [cv_ref[:, head_cols(h)].astype(BF16) for h in range(N_DIFF_HEADS)]
    v_n = [vn_ref[:, head_cols(h)].astype(BF16) for h in range(N_DIFF_HEADS)]
    o_maps = {}
    for mp in range(2):
        for pair in range(N_DIFF_HEADS // 2):
            heads = (2 * pair, 2 * pair + 1)
            ks = head_cols(2 * mp + pair)
            qq = jnp.concatenate([masked_q(h, mp) for h in heads], axis=0)
            s_c = lax.dot_general(qq, ck_ref[:, ks].astype(BF16), _NT, preferred_element_type=F32)
            s_n = lax.dot_general(qq, kn_ref[:, ks].astype(BF16), _NT, preferred_element_type=F32)
            s_n = jnp.where(visible2, s_n, -jnp.inf)
            mx = jnp.maximum(jnp.max(s_c, axis=-1, keepdims=True), jnp.max(s_n, axis=-1, keepdims=True))
            p_c = jnp.exp2(s_c - mx)
            p_n = jnp.exp2(s_n - mx)
            inv_l = 1.0 / (jnp.sum(p_c, axis=-1, keepdims=True) + jnp.sum(p_n, axis=-1, keepdims=True))
            p_c, p_n = p_c.astype(BF16), p_n.astype(BF16)
            for n, h in enumerate(heads):
                rows = slice(n * n_new, (n + 1) * n_new)
                o = jnp.dot(p_c[rows], v_c[h], preferred_element_type=F32) + \
                    jnp.dot(p_n[rows], v_n[h], preferred_element_type=F32)
                o_maps[h, mp] = o * inv_l[rows]
    for h in range(N_DIFF_HEADS):
        o = o_maps[h, 0] - lam * o_maps[h, 1]
        ms = jnp.mean(o * o, axis=-1, keepdims=True)
        o_ref[:, head_cols(h)] = (o * lax.rsqrt(ms + SUBLN_EPS) * sg * (1.0 - lam_init)).astype(BF16)


def _sample_attn(q, k_new, v_new, cache_k, cache_v, layer, lam_q, lam_k, subln_g, lam_init):
    batch, n_new, _ = q.shape
    past = cache_k.shape[2]
    w = DIFF_WIDTH
    new = pl.BlockSpec((None, n_new, w), lambda b: (b, 0, 0))
    cached = pl.BlockSpec((None, None, past, w), lambda b: (layer, b, 0, 0))
    return pl.pallas_call(
        functools.partial(_sample_attn_kernel, past=past, n_new=n_new, lam_init=lam_init),
        grid=(batch,),
        in_specs=[new, new, new, cached, cached,
                  _const_spec(lam_q.shape), _const_spec(lam_k.shape), _const_spec(subln_g.shape)],
        out_specs=new,
        out_shape=jax.ShapeDtypeStruct((batch, n_new, w), BF16),
        compiler_params=_params("arbitrary"),
        name="sample_attn",
    )(q, k_new, v_new, cache_k, cache_v, lam_q, lam_k, subln_g)


def _mem_kv_kernel(m_ref, g_ref, w_ref, k_ref, v_ref, kb_ref, vb_ref):
    d = m_ref.shape[-1]
    z = jnp.dot(_rms_bf16(m_ref[...], g_ref[...], NORM_EPS), w_ref[...], preferred_element_type=F32)
    k_ref[...] = z[:, :d]
    v_ref[...] = z[:, d:]
    kb_ref[...] = z[:, :d].astype(BF16)
    vb_ref[...] = z[:, d:].astype(BF16)


def _mem_kv(mem, g, w_kv):
    batch, n_mem, d = mem.shape
    rows = batch * n_mem
    tm = min(ROW_TILE, rows)
    assert rows % tm == 0
    tok = pl.BlockSpec((tm, d), lambda i: (i, 0))
    outs = pl.pallas_call(
        _mem_kv_kernel,
        grid=(rows // tm,),
        in_specs=[tok, _const_spec(g.shape), _const_spec(w_kv.shape)],
        out_specs=[tok] * 4,
        out_shape=[jax.ShapeDtypeStruct((rows, d), F32)] * 2 + [jax.ShapeDtypeStruct((rows, d), BF16)] * 2,
        compiler_params=_params("arbitrary"),
        name="mem_kv",
    )(mem.reshape(rows, d), g, w_kv)
    return [o.reshape(batch, n_mem, d) for o in outs]


def _mix_kernel(x_ref, a_ref, p_ref, wo_ref, g_ref, wq_ref, mk_ref, mv_ref, wox_ref, o_ref, *, bb, ts):
    d = x_ref.shape[-1]
    m = bb * ts
    hd = d // N_X_HEADS
    ap = jnp.concatenate([a_ref[...].reshape(m, -1), p_ref[...].reshape(m, -1)], axis=-1)
    x1 = x_ref[...].reshape(m, d) + jnp.dot(ap, wo_ref[...], preferred_element_type=F32)
    h = _rms_bf16(x1, g_ref[...], NORM_EPS)
    q = jnp.dot(h, wq_ref[...], preferred_element_type=F32).astype(BF16)
    units = [(i, hh) for i in range(bb) for hh in range(N_X_HEADS)]
    scores = lambda i, hh: lax.dot_general(q[i * ts:(i + 1) * ts, hh * hd:(hh + 1) * hd],
                                           mk_ref[i, :, hh * hd:(hh + 1) * hd], _NT, preferred_element_type=F32)
    outs = []
    s_next = scores(*units[0])
    for n, (i, hh) in enumerate(units):
        s = s_next
        if n + 1 < len(units):
            s_next = scores(*units[n + 1])
        p = jnp.exp(s - jnp.max(s, axis=-1, keepdims=True))
        inv_l = 1.0 / jnp.sum(p, axis=-1, keepdims=True)
        o = jnp.dot(p.astype(BF16), mv_ref[i, :, hh * hd:(hh + 1) * hd], preferred_element_type=F32) * inv_l
        outs.append(o.astype(BF16))
    rows = [jnp.concatenate(outs[i * N_X_HEADS:(i + 1) * N_X_HEADS], axis=-1) for i in range(bb)]
    o_all = rows[0] if bb == 1 else jnp.concatenate(rows, axis=0)
    o_ref[...] = (x1 + jnp.dot(o_all, wox_ref[...], preferred_element_type=F32)).reshape(bb, ts, d)


def _mix(x, a, pool_y, w_out, g, wq, mk, mv, layer, wo_x):
    batch, seq, d = x.shape
    bb, ts = _row_tiling(batch, seq, WIDE_ROW_TILE if seq >= WIDE_ROW_TILE else None)
    n_mem = mk.shape[2]
    tok = lambda width: pl.BlockSpec((bb, ts, width), lambda b, s: (b, s, 0))
    mem = pl.BlockSpec((None, bb, n_mem, d), lambda b, s: (layer, b, 0, 0))
    return pl.pallas_call(
        functools.partial(_mix_kernel, bb=bb, ts=ts),
        grid=(batch // bb, seq // ts),
        in_specs=[tok(d), tok(a.shape[-1]), tok(pool_y.shape[-1]), _const_spec(w_out.shape), _const_spec(g.shape),
                  _const_spec(wq.shape), mem, mem, _const_spec(wo_x.shape)],
        out_specs=tok(d),
        out_shape=jax.ShapeDtypeStruct(x.shape, F32),
        compiler_params=_params("arbitrary", "arbitrary"),
        name="mix",
    )(x, a, pool_y, w_out, g, wq, mk, mv, wo_x)


def _mlp_kernel(x_ref, g_ref, wu_ref, wd_ref, fg_ref, o_ref, *, bb, ts, final):
    d = x_ref.shape[-1]
    m = bb * ts
    x = x_ref[...].reshape(m, d)
    h = _rms_bf16(x, g_ref[...], NORM_EPS)
    ck = wu_ref.shape[1] // MLP_CHUNKS
    acc = x
    for j in range(MLP_CHUNKS):
        up = jnp.maximum(jnp.dot(h, wu_ref[:, j * ck:(j + 1) * ck], preferred_element_type=F32), 0.0)
        acc = acc + jnp.dot((up * up).astype(BF16), wd_ref[j * ck:(j + 1) * ck, :], preferred_element_type=F32)
    if final:
        ms = jnp.mean(acc * acc, axis=-1, keepdims=True)
        acc = acc * lax.rsqrt(ms + NORM_EPS) * fg_ref[...]
    o_ref[...] = acc.reshape(bb, ts, d)


def _mlp(x, g, w_up, w_down, final_g, final):
    batch, seq, d = x.shape
    bb, ts = _row_tiling(batch, seq, WIDE_ROW_TILE if seq >= WIDE_ROW_TILE else None)
    assert w_up.shape[1] % MLP_CHUNKS == 0
    tok = pl.BlockSpec((bb, ts, d), lambda b, s: (b, s, 0))
    return pl.pallas_call(
        functools.partial(_mlp_kernel, bb=bb, ts=ts, final=final),
        grid=(batch // bb, seq // ts),
        in_specs=[tok, _const_spec(g.shape), _const_spec(w_up.shape), _const_spec(w_down.shape),
                  _const_spec(final_g.shape)],
        out_specs=tok,
        out_shape=jax.ShapeDtypeStruct(x.shape, F32),
        compiler_params=_params("arbitrary", "arbitrary"),
        name="mlp",
    )(x, g, w_up, w_down, final_g)


def _head_major(w):
    d = w.shape[0]
    return w.reshape(d, 2, N_DIFF_HEADS, DIFF_DK).transpose(0, 2, 1, 3).reshape(d, DIFF_WIDTH)


def _in_weights(w_in):
    w = DIFF_WIDTH
    wq, wk, wv, wu = w_in[:, :w], w_in[:, w:2 * w], w_in[:, 2 * w:3 * w], w_in[:, 3 * w:]
    wq = _head_major(wq) * (DIFF_DK ** -0.5 * LOG2E)
    cols_p = jnp.concatenate([wv, wk, wu], axis=1).astype(BF16)
    rows_t = wq.T.astype(BF16)
    cols_s = jnp.concatenate([wq, wv, wk, wu], axis=1).astype(BF16)
    return cols_p, rows_t, cols_s


def kernel(x_prompt, x_sample, cache_k, cache_v, state_pool, cache_mem_k, cache_mem_v, mem_prompt,
           norm_mix_g, w_in, lam_q, lam_k, subln_g, w_pool, pool_scale, w_out,
           norm_x_g, norm_mem_g, wq_x, wk_x, wv_x, wo_x, norm_mlp_g, w_up, w_down, final_g):
    depth = w_in.shape[0]
    batch, seq, d = x_prompt.shape
    dec_batch, dec_seq, _ = x_sample.shape
    past = cache_k.shape[2]
    n_mem = mem_prompt.shape[1]
    hd = d // N_X_HEADS

    ck = cache_k.reshape(depth, dec_batch, past, DIFF_WIDTH)
    cv = cache_v.reshape(depth, dec_batch, past, DIFF_WIDTH)
    cmk = cache_mem_k.reshape(depth, dec_batch, n_mem, d).astype(BF16)
    cmv = cache_mem_v.reshape(depth, dec_batch, n_mem, d).astype(BF16)
    hist_s = jnp.pad(state_pool, ((0, 0), (0, 0), (HIST_ROWS - POOL_HIST, 0), (0, 0)))
    hist_p = jnp.zeros((batch, HIST_ROWS, POOL_WIDTH), F32)
    row = lambda v: v.reshape(1, -1)
    fg = row(final_g)

    xp, xs = x_prompt, x_sample
    outs = [[] for _ in range(6)]
    kv_prompt = None
    for l in range(depth):
        lam_init = 0.8 - 0.6 * math.exp(-0.3 * l)
        last = l == depth - 1
        w_cols_p, w_rows_t, w_cols_s = _in_weights(w_in[l])
        wp = w_pool[l].astype(BF16)
        ps = row(pool_scale[l])
        g_mix, g_x, g_mlp, sg = row(norm_mix_g[l]), row(norm_x_g[l]), row(norm_mlp_g[l]), row(subln_g[l])
        w_o = w_out[l].astype(BF16)
        wq = (wq_x[l] * hd ** -0.5).astype(BF16)
        wo = wo_x[l].astype(BF16)
        wu, wd = w_up[l].astype(BF16), w_down[l].astype(BF16)
        w_kv = jnp.concatenate([wk_x[l], wv_x[l]], axis=1).astype(BF16)

        qt, vta, ka, *kv_prompt, py, ph = _in_proj_prompt(xp, g_mix, w_cols_p, w_rows_t, hist_p, wp, ps, l, depth,
                                                          kv_prompt)
        a = _prompt_attn(qt, ka, vta, lam_q[l], lam_k[l], sg.reshape(-1, 1), lam_init)
        mk, mv, mkb, mvb = _mem_kv(mem_prompt, row(norm_mem_g[l]), w_kv)
        xp = _mix(xp, a, py, w_o, g_x, wq, mkb[None], mvb[None], 0, wo)
        xp = _mlp(xp, g_mlp, wu, wd, fg, last)
        for dst, val in zip(outs[:3], (ph[:, HIST_ROWS - POOL_HIST:], mk, mv)):
            dst.append(val)

        q, ko, vo, py, ph = _in_proj_sample(xs, g_mix, w_cols_s, hist_s[l], wp, ps, past)
        a = _sample_attn(q, ko, vo, ck, cv, l, lam_q[l], lam_k[l], sg, lam_init)
        xs = _mix(xs, a, py, w_o, g_x, wq, cmk, cmv, l, wo)
        xs = _mlp(xs, g_mlp, wu, wd, fg, last)
        for dst, val in zip(outs[3:], (ko, vo, ph[:, HIST_ROWS - POOL_HIST:])):
            dst.append(val)

    kp, vp = kv_prompt
    pp, mkp, mvp, ks, vs, pps = [jnp.stack(o) for o in outs]
    return (xp, xs,
            kp.reshape(depth, batch, seq, 2, N_DIFF_HEADS, DIFF_DK),
            vp.reshape(depth, batch, seq, N_DIFF_HEADS, DIFF_DV),
            pp,
            mkp.reshape(depth, batch, n_mem, N_X_HEADS, hd),
            mvp.reshape(depth, batch, n_mem, N_X_HEADS, hd),
            ks.reshape(depth, dec_batch, dec_seq, 2, N_DIFF_HEADS, DIFF_DK),
            vs.reshape(depth, dec_batch, dec_seq, N_DIFF_HEADS, DIFF_DV),
            pps)
```

```python
import functools
import math

import jax
import jax.numpy as jnp
from jax import lax
from jax.experimental import pallas as pl
from jax.experimental.pallas import tpu as pltpu

F32 = jnp.float32
BF16 = jnp.bfloat16

CHUNK = 64
CHUNK_SHIFT = CHUNK.bit_length() - 1
N_DIFF_HEADS = 4
DIFF_DK = 64
DIFF_DV = 2 * DIFF_DK
DIFF_WIDTH = N_DIFF_HEADS * DIFF_DV
V_AUG_ROWS = DIFF_DV + 16
POOL_WINDOWS = (2, 4, 8, 16)
POOL_GC = 128
POOL_WIDTH = len(POOL_WINDOWS) * POOL_GC
assert all(w & (w - 1) == 0 for w in POOL_WINDOWS)
POOL_HIST = max(POOL_WINDOWS) - 1
HIST_ROWS = 16
N_X_HEADS = 4
NORM_EPS = 1e-6
SUBLN_EPS = 1e-5
LOG2E = 1.4426950408889634

ROW_TILE = 512
WIDE_ROW_TILE = 1024
MLP_CHUNKS = 4
V7X_VMEM_LIMIT = 56 * 2**20

_NT = (((1,), (1,)), ((), ()))


def _rms_bf16(x, g, eps):
    ms = jnp.mean(x * x, axis=-1, keepdims=True)
    return (x * lax.rsqrt(ms + eps) * g).astype(BF16)


def _row_tiling(batch, seq, rows=None):
    rows = ROW_TILE if rows is None else rows
    if seq >= rows:
        assert seq % rows == 0
        return 1, rows
    bb = max(1, min(batch, rows // seq))
    assert batch % bb == 0 and seq % 8 == 0
    return bb, seq


def _params(*semantics):
    return pltpu.CompilerParams(dimension_semantics=semantics, vmem_limit_bytes=V7X_VMEM_LIMIT)


def _const_spec(shape):
    nd = len(shape)
    return pl.BlockSpec(shape, lambda *_: (0,) * nd, pipeline_mode=pl.Buffered(1))


def _project(x_ref, g_ref, w_ref, ko_ref, vo_ref, ext_ref, hist_ref, ph_ref, wp_ref, ps_ref, py_ref, bb, ts, start):
    d = x_ref.shape[-1]
    m = bb * ts
    w = DIFF_WIDTH
    h = _rms_bf16(x_ref[...].reshape(m, d), g_ref[...], NORM_EPS)
    u = jnp.dot(h, w_ref[:, w_ref.shape[1] - POOL_WIDTH:], preferred_element_type=F32)

    @pl.when(pl.program_id(1) == 0)
    def _():
        ext_ref[:, 0:HIST_ROWS, :] = hist_ref[...]

    ext_ref[:, HIST_ROWS:HIST_ROWS + ts, :] = u.reshape(bb, ts, POOL_WIDTH)
    groups = _pool_windows(ext_ref, ph_ref, bb, ts, start)
    half = len(groups) // 2
    n_lead = w_ref.shape[1] - 2 * w - POOL_WIDTH
    first = jnp.dot(h, w_ref[:, :n_lead], preferred_element_type=F32) if n_lead else None
    _pool_project(groups[:half], 0, wp_ref, ps_ref, py_ref, bb, ts)
    v = jnp.dot(h, w_ref[:, n_lead:n_lead + w], preferred_element_type=F32)
    vo_ref[...] = v.reshape(bb, ts, w)
    _pool_project(groups[half:], half, wp_ref, ps_ref, py_ref, bb, ts)
    k = jnp.dot(h, w_ref[:, n_lead + w:n_lead + 2 * w], preferred_element_type=F32)
    ko_ref[...] = k.reshape(bb, ts, w)
    return h, first, v, k


def _pool_windows(ext_ref, ph_ref, bb, ts, start):
    pos = start + pl.program_id(1) * ts + lax.broadcasted_iota(jnp.int32, (ts, 1), 0)
    groups = []
    for gi, win_len in enumerate(POOL_WINDOWS):
        cols = slice(gi * POOL_GC, (gi + 1) * POOL_GC)
        inv_cnt = 1.0 / jnp.minimum(win_len, pos + 1).astype(F32)
        pooled = []
        for i in range(bb):
            win = ext_ref[i, :, cols]
            k = 1
            while k < win_len:
                win = win + pltpu.roll(win, k, axis=0)
                k *= 2
            u = ext_ref[i, HIST_ROWS:HIST_ROWS + ts, cols]
            pooled.append((win[HIST_ROWS:] * inv_cnt - u).astype(BF16))
        groups.append(pooled[0] if bb == 1 else jnp.concatenate(pooled, axis=0))
    tail = ext_ref[:, ts:ts + HIST_ROWS, :]
    ph_ref[...] = tail
    ext_ref[:, 0:HIST_ROWS, :] = tail
    return groups


def _pool_project(groups, first_group, wp_ref, ps_ref, py_ref, bb, ts):
    for gi, pooled in enumerate(groups, first_group):
        cols = slice(gi * POOL_GC, (gi + 1) * POOL_GC)
        y = jnp.dot(pooled, wp_ref[gi], preferred_element_type=F32) * ps_ref[:, cols]
        py_ref[:, :, cols] = y.astype(BF16).reshape(bb, ts, POOL_GC)


def _in_proj_prompt_kernel(x_ref, g_ref, w_ref, wt_ref, hist_ref, wp_ref, ps_ref, *refs, ts, start):
    qt_ref, vta_ref, ka_ref, ko_ref, vo_ref, py_ref, ph_ref, ext_ref = refs[-8:]
    if len(refs) == 8:
        for ref in (ko_ref, vo_ref):
            if ref.shape[0] > 1:
                ref[1:] = jnp.zeros((ref.shape[0] - 1,) + ref.shape[1:], F32)
        ko_ref, vo_ref = ko_ref.at[0], vo_ref.at[0]
    h, _, v, k = _project(x_ref, g_ref, w_ref, ko_ref, vo_ref, ext_ref, hist_ref, ph_ref, wp_ref, ps_ref, py_ref,
                       1, ts, start)
    blocks = [k[:, j * DIFF_DV:(j + 1) * DIFF_DV] for j in range(DIFF_WIDTH // DIFF_DV)]
    low_half = lax.broadcasted_iota(jnp.int32, (ts, DIFF_DV), 1) < DIFF_DK
    for hd in range(N_DIFF_HEADS):
        map0, map1 = blocks[hd // 2], blocks[N_DIFF_HEADS // 2 + hd // 2]
        if hd % 2 == 0:
            k_hm = jnp.where(low_half, map0, pltpu.roll(map1, DIFF_DK, axis=1))
        else:
            k_hm = jnp.where(low_half, pltpu.roll(map0, DIFF_DK, axis=1), map1)
        ka_ref[:, hd * DIFF_DV:(hd + 1) * DIFF_DV] = k_hm.astype(BF16)
    qt_ref[...] = lax.dot_general(wt_ref[...], h, _NT, preferred_element_type=F32).astype(BF16)
    ones = (lax.broadcasted_iota(jnp.int32, (V_AUG_ROWS - DIFF_DV, ts), 0) == 0).astype(BF16)
    for hd in range(N_DIFF_HEADS):
        vta_ref[hd * V_AUG_ROWS:hd * V_AUG_ROWS + DIFF_DV, :] = v[:, hd * DIFF_DV:(hd + 1) * DIFF_DV].T.astype(BF16)
        vta_ref[hd * V_AUG_ROWS + DIFF_DV:(hd + 1) * V_AUG_ROWS, :] = ones


def _in_proj_sample_kernel(x_ref, g_ref, w_ref, hist_ref, wp_ref, ps_ref,
                           q_ref, ko_ref, vo_ref, py_ref, ph_ref, ext_ref, *, bb, ts, start):
    _, q, _, _ = _project(x_ref, g_ref, w_ref, ko_ref, vo_ref, ext_ref, hist_ref, ph_ref, wp_ref, ps_ref, py_ref,
                       bb, ts, start)
    q_ref[...] = q.astype(BF16).reshape(bb, ts, DIFF_WIDTH)


def _in_proj_common(batch, seq, bb, ts):
    assert seq >= HIST_ROWS
    tok = lambda width: pl.BlockSpec((bb, ts, width), lambda b, s: (b, s, 0))
    hist_spec = pl.BlockSpec((bb, HIST_ROWS, POOL_WIDTH), lambda b, s: (b, 0, 0))
    tok_shape = lambda width, dt: jax.ShapeDtypeStruct((batch, seq, width), dt)
    pool_specs = [tok(POOL_WIDTH), hist_spec]
    pool_shapes = [tok_shape(POOL_WIDTH, BF16), jax.ShapeDtypeStruct((batch, HIST_ROWS, POOL_WIDTH), F32)]
    return tok, hist_spec, tok_shape, pool_specs, pool_shapes


def _in_proj_prompt(x, g, w_cols, w_rows_t, hist, w_pool, pool_scale, layer, depth, kv_all):
    batch, seq, d = x.shape
    bb, ts = _row_tiling(batch, seq)
    assert bb == 1
    w = DIFF_WIDTH
    tok, hist_spec, tok_shape, pool_specs, pool_shapes = _in_proj_common(batch, seq, bb, ts)
    va_rows = N_DIFF_HEADS * V_AUG_ROWS
    tile_t = lambda rows: pl.BlockSpec((None, None, rows, ts), lambda b, s: (b, s, 0, 0))
    tile_t_shape = lambda rows: jax.ShapeDtypeStruct((batch, seq // ts, rows, ts), BF16)
    if kv_all is None:
        assert layer == 0
        layered = pl.BlockSpec((depth, bb, ts, w), lambda b, s: (0, b, s, 0))
    else:
        layered = pl.BlockSpec((None, bb, ts, w), lambda b, s: (layer, b, s, 0))
    in_specs = [tok(d), _const_spec(g.shape), _const_spec(w_cols.shape), _const_spec(w_rows_t.shape),
                hist_spec, _const_spec(w_pool.shape), _const_spec(pool_scale.shape)]
    args = [x, g, w_cols, w_rows_t, hist, w_pool, pool_scale]
    aliases = {}
    if kv_all is not None:
        aliases = {len(args): 3, len(args) + 1: 4}
        in_specs += [pl.BlockSpec(memory_space=pl.ANY)] * 2
        args += list(kv_all)
    return pl.pallas_call(
        functools.partial(_in_proj_prompt_kernel, ts=ts, start=0),
        grid=(batch // bb, seq // ts),
        in_specs=in_specs,
        out_specs=[tile_t(w), tile_t(va_rows),
                   pl.BlockSpec((None, ts, w), lambda b, s: (b, s, 0)), layered, layered] + pool_specs,
        out_shape=[tile_t_shape(w), tile_t_shape(va_rows), tok_shape(w, BF16),
                   jax.ShapeDtypeStruct((depth, batch, seq, w), F32),
                   jax.ShapeDtypeStruct((depth, batch, seq, w), F32)] + pool_shapes,
        input_output_aliases=aliases,
        scratch_shapes=[pltpu.VMEM((bb, HIST_ROWS + ts, POOL_WIDTH), F32)],
        compiler_params=_params("arbitrary", "arbitrary"),
        name="in_proj_prompt",
    )(*args)


def _in_proj_sample(x, g, w_cols, hist, w_pool, pool_scale, start):
    batch, seq, d = x.shape
    bb, ts = _row_tiling(batch, seq)
    w = DIFF_WIDTH
    tok, hist_spec, tok_shape, pool_specs, pool_shapes = _in_proj_common(batch, seq, bb, ts)
    return pl.pallas_call(
        functools.partial(_in_proj_sample_kernel, bb=bb, ts=ts, start=start),
        grid=(batch // bb, seq // ts),
        in_specs=[tok(d), _const_spec(g.shape), _const_spec(w_cols.shape),
                  hist_spec, _const_spec(w_pool.shape), _const_spec(pool_scale.shape)],
        out_specs=[tok(w), tok(w), tok(w)] + pool_specs,
        out_shape=[tok_shape(w, BF16), tok_shape(w, F32), tok_shape(w, F32)] + pool_shapes,
        scratch_shapes=[pltpu.VMEM((bb, HIST_ROWS + ts, POOL_WIDTH), F32)],
        compiler_params=_params("arbitrary", "arbitrary"),
        name="in_proj_sample",
    )(x, g, w_cols, hist, w_pool, pool_scale)


def _lambda(lq_ref, lk_ref, lam_init):
    e = jnp.exp(jnp.sum(lq_ref[...] * lk_ref[...], axis=-1, keepdims=True))
    return e[0:1] - e[1:2] + lam_init


def _prompt_attn_kernel(qt_ref, k_ref, vt_ref, lq_ref, lk_ref, sg_ref, o_ref, s_ref, acc_ref, *, t, n_blk, lam_init):
    lam = _lambda(lq_ref, lk_ref, lam_init)

    def masked_q(i, mp):
        zero = jnp.zeros((DIFF_DK, t), BF16)
        if mp == 0:
            return jnp.concatenate([qt_ref[i, :DIFF_DK, :], zero], axis=0)
        return jnp.concatenate([zero, qt_ref[i, DIFF_DK:, :]], axis=0)

    def scores(i, j, slot, mp):
        k_blk = k_ref[pl.ds(pl.multiple_of(j * t, t), t), :]
        s_ref[slot, mp] = jnp.dot(k_blk, masked_q(i, mp), preferred_element_type=F32)

    def update(j, slot, mp, m, masked):
        s = s_ref[slot, mp]
        if masked:
            key = lax.broadcasted_iota(jnp.int32, s.shape, 0)
            qry = lax.broadcasted_iota(jnp.int32, s.shape, 1)
            s = jnp.where(lax.shift_right_logical(key, CHUNK_SHIFT) <= lax.shift_right_logical(qry, CHUNK_SHIFT),
                          s, -jnp.inf)
        m_new = jnp.maximum(m, jnp.max(s, axis=0, keepdims=True))
        alpha = jnp.exp2(m - m_new)
        p = jnp.exp2(s - m_new)
        acc_ref[mp] = alpha * acc_ref[mp] + jnp.dot(vt_ref[j], p.astype(BF16), preferred_element_type=F32)
        return m_new

    def chunk_masked(s, key0, qry0):
        key = lax.broadcasted_iota(jnp.int32, s.shape, 0) + key0
        qry = lax.broadcasted_iota(jnp.int32, s.shape, 1) + qry0
        return jnp.where(lax.shift_right_logical(key, CHUNK_SHIFT) <= lax.shift_right_logical(qry, CHUNK_SHIFT),
                         s, -jnp.inf)

    def update_diagonal(i, slot, mp, m):
        hf = t // 2
        if hf % DIFF_DV or hf % CHUNK:
            return update(i, slot, mp, m, True)
        s_top = chunk_masked(s_ref[slot, mp, :hf, :], 0, 0)
        s_bot = chunk_masked(s_ref[slot, mp, hf:, hf:], hf, hf)
        m_top = jnp.max(s_top, axis=0, keepdims=True)
        m_blk = jnp.concatenate([m_top[:, :hf], jnp.maximum(m_top[:, hf:], jnp.max(s_bot, axis=0, keepdims=True))],
                                axis=1)
        m_new = jnp.maximum(m, m_blk)
        alpha = jnp.exp2(m - m_new)
        p_top = jnp.exp2(s_top - m_new).astype(BF16)
        p_bot = jnp.exp2(s_bot - m_new[:, hf:]).astype(BF16)
        acc_ref[mp] = alpha * acc_ref[mp] + jnp.dot(vt_ref[i, :, :hf], p_top, preferred_element_type=F32)
        acc_ref[mp, :, hf:] = acc_ref[mp, :, hf:] + jnp.dot(vt_ref[i, :, hf:], p_bot, preferred_element_type=F32)
        return m_new

    def step(i, j, cur, nxt, m0, m1):
        scores(i, j + 1, nxt, 0)
        m0 = update(j, cur, 0, m0, False)
        scores(i, j + 1, nxt, 1)
        m1 = update(j, cur, 1, m1, False)
        return m0, m1

    def finish(i, slot, m0, m1, next_first):
        if next_first is not None:
            scores(i + 1, 0, next_first, 0)
        update_diagonal(i, slot, 0, m0)
        if next_first is not None:
            scores(i + 1, 0, next_first, 1)
        update_diagonal(i, slot, 1, m1)
        normalised = lambda acc: acc[:DIFF_DV] * (1.0 / acc[DIFF_DV:DIFF_DV + 1])
        o = normalised(acc_ref[0]) - lam * normalised(acc_ref[1])
        ms = jnp.mean(o * o, axis=0, keepdims=True)
        y = o * lax.rsqrt(ms + SUBLN_EPS) * sg_ref[...] * (1.0 - lam_init)
        o_ref[pl.ds(pl.multiple_of(i * t, t), t), :] = y.T.astype(BF16)

    def start():
        acc_ref[...] = jnp.zeros(acc_ref.shape, F32)
        return jnp.full((1, t), -jnp.inf, F32)

    first_slot = lambda i_mod_2: 3 if i_mod_2 else 2

    def query_block(i, rem, last):
        init = start()
        m = step(i, 0, first_slot((rem + 1) % 2), 0, init, init)
        pair = lambda j, c: step(i, j + 1, 1, 0, *step(i, j, 0, 1, *c))
        n_quads = (i - 1) // 4
        m = lax.fori_loop(0, n_quads, lambda p, c: pair(4 * p + 3, pair(4 * p + 1, c)), m)
        for n in range(rem):
            m = step(i, 4 * n_quads + 1 + n, n % 2, 1 - n % 2, *m)
        finish(i, rem % 2, *m, None if last else first_slot(rem % 2))

    scores(0, 0, 2, 0)
    scores(0, 0, 2, 1)
    init = start()
    finish(0, 2, init, init, first_slot(1) if n_blk > 1 else None)
    n_tail = (n_blk - 2) % 4 + 1 if n_blk > 1 else 0
    n_groups = (n_blk - 1 - n_tail) // 4

    def four_query_blocks(g, carry):
        for rem in range(4):
            query_block(4 * g + 1 + rem, rem, False)
        return carry

    lax.fori_loop(0, n_groups, four_query_blocks, 0)
    for i in range(n_blk - n_tail, n_blk):
        query_block(i, (i - 1) % 4, i == n_blk - 1)


def _prompt_attn(qt, k, vta, lam_q, lam_k, subln_g_col, lam_init):
    batch, n_blk, _, t = qt.shape
    seq = n_blk * t
    assert t % CHUNK == 0 and (t & (t - 1)) == 0
    return pl.pallas_call(
        functools.partial(_prompt_attn_kernel, t=t, n_blk=n_blk, lam_init=lam_init),
        grid=(batch, N_DIFF_HEADS),
        in_specs=[pl.BlockSpec((None, n_blk, DIFF_DV, t), lambda b, h: (b, 0, h, 0)),
                  pl.BlockSpec((None, seq, DIFF_DV), lambda b, h: (b, 0, h)),
                  pl.BlockSpec((None, n_blk, V_AUG_ROWS, t), lambda b, h: (b, 0, h, 0)),
                  _const_spec(lam_q.shape), _const_spec(lam_k.shape), _const_spec(subln_g_col.shape)],
        out_specs=pl.BlockSpec((None, seq, DIFF_DV), lambda b, h: (b, 0, h)),
        out_shape=jax.ShapeDtypeStruct((batch, seq, DIFF_WIDTH), BF16),
        scratch_shapes=[pltpu.VMEM((4, 2, t, t), F32),
                        pltpu.VMEM((2, V_AUG_ROWS, t), F32)],
        compiler_params=_params("arbitrary", "arbitrary"),
        name="prompt_attn",
    )(qt, k, vta, lam_q, lam_k, subln_g_col)


def _sample_attn_kernel(q_ref, kn_ref, vn_ref, ck_ref, cv_ref, lq_ref, lk_ref, sg_ref, o_ref,
                        *, past, n_new, lam_init):
    lam = _lambda(lq_ref, lk_ref, lam_init)
    sg = sg_ref[...]
    lane = lax.broadcasted_iota(jnp.int32, (n_new, DIFF_DV), 1)
    r = lax.broadcasted_iota(jnp.int32, (n_new, n_new), 0)
    c = lax.broadcasted_iota(jnp.int32, (n_new, n_new), 1)
    visible = lax.shift_right_logical(past + c, CHUNK_SHIFT) <= lax.shift_right_logical(past + r, CHUNK_SHIFT)
    visible2 = jnp.concatenate([visible, visible], axis=0)
    head_cols = lambda h: slice(h * DIFF_DV, (h + 1) * DIFF_DV)

    def masked_q(h, mp):
        q_blk = q_ref[:, head_cols(h)].astype(F32)
        qm = q_blk if mp == h % 2 else pltpu.roll(q_blk, DIFF_DK, axis=1)
        return jnp.where((lane >= DIFF_DK) == bool(h % 2), qm, 0.0).astype(BF16)

    v_c = [cv_ref[:, head_cols(h)].astype(BF16) for h in range(N_DIFF_HEADS)]
    v_n = [vn_ref[:, head_cols(h)].astype(BF16) for h in range(N_DIFF_HEADS)]
    o_maps = {}
    for mp in range(2):
        for pair in range(N_DIFF_HEADS // 2):
            heads = (2 * pair, 2 * pair + 1)
            ks = head_cols(2 * mp + pair)
            qq = jnp.concatenate([masked_q(h, mp) for h in heads], axis=0)
            s_c = lax.dot_general(qq, ck_ref[:, ks].astype(BF16), _NT, preferred_element_type=F32)
            s_n = lax.dot_general(qq, kn_ref[:, ks].astype(BF16), _NT, preferred_element_type=F32)
            s_n = jnp.where(visible2, s_n, -jnp.inf)
            mx = jnp.maximum(jnp.max(s_c, axis=-1, keepdims=True), jnp.max(s_n, axis=-1, keepdims=True))
            p_c = jnp.exp2(s_c - mx)
            p_n = jnp.exp2(s_n - mx)
            inv_l = 1.0 / (jnp.sum(p_c, axis=-1, keepdims=True) + jnp.sum(p_n, axis=-1, keepdims=True))
            p_c, p_n = p_c.astype(BF16), p_n.astype(BF16)
            for n, h in enumerate(heads):
                rows = slice(n * n_new, (n + 1) * n_new)
                o = jnp.dot(p_c[rows], v_c[h], preferred_element_type=F32) + \
                    jnp.dot(p_n[rows], v_n[h], preferred_element_type=F32)
                o_maps[h, mp] = o * inv_l[rows]
    for h in range(N_DIFF_HEADS):
        o = o_maps[h, 0] - lam * o_maps[h, 1]
        ms = jnp.mean(o * o, axis=-1, keepdims=True)
        o_ref[:, head_cols(h)] = (o * lax.rsqrt(ms + SUBLN_EPS) * sg * (1.0 - lam_init)).astype(BF16)


def _sample_attn(q, k_new, v_new, cache_k, cache_v, layer, lam_q, lam_k, subln_g, lam_init):
    batch, n_new, _ = q.shape
    past = cache_k.shape[2]
    w = DIFF_WIDTH
    new = pl.BlockSpec((None, n_new, w), lambda b: (b, 0, 0))
    cached = pl.BlockSpec((None, None, past, w), lambda b: (layer, b, 0, 0))
    return pl.pallas_call(
        functools.partial(_sample_attn_kernel, past=past, n_new=n_new, lam_init=lam_init),
        grid=(batch,),
        in_specs=[new, new, new, cached, cached,
                  _const_spec(lam_q.shape), _const_spec(lam_k.shape), _const_spec(subln_g.shape)],
        out_specs=new,
        out_shape=jax.ShapeDtypeStruct((batch, n_new, w), BF16),
        compiler_params=_params("arbitrary"),
        name="sample_attn",
    )(q, k_new, v_new, cache_k, cache_v, lam_q, lam_k, subln_g)


def _mem_kv_kernel(m_ref, g_ref, w_ref, k_ref, v_ref, kb_ref, vb_ref):
    d = m_ref.shape[-1]
    z = jnp.dot(_rms_bf16(m_ref[...], g_ref[...], NORM_EPS), w_ref[...], preferred_element_type=F32)
    k_ref[...] = z[:, :d]
    v_ref[...] = z[:, d:]
    kb_ref[...] = z[:, :d].astype(BF16)
    vb_ref[...] = z[:, d:].astype(BF16)


def _mem_kv(mem, g, w_kv):
    batch, n_mem, d = mem.shape
    rows = batch * n_mem
    tm = min(ROW_TILE, rows)
    assert rows % tm == 0
    tok = pl.BlockSpec((tm, d), lambda i: (i, 0))
    outs = pl.pallas_call(
        _mem_kv_kernel,
        grid=(rows // tm,),
        in_specs=[tok, _const_spec(g.shape), _const_spec(w_kv.shape)],
        out_specs=[tok] * 4,
        out_shape=[jax.ShapeDtypeStruct((rows, d), F32)] * 2 + [jax.ShapeDtypeStruct((rows, d), BF16)] * 2,
        compiler_params=_params("arbitrary"),
        name="mem_kv",
    )(mem.reshape(rows, d), g, w_kv)
    return [o.reshape(batch, n_mem, d) for o in outs]


def _mix_kernel(x_ref, a_ref, p_ref, wo_ref, g_ref, wq_ref, mk_ref, mv_ref, wox_ref, o_ref, *, bb, ts):
    d = x_ref.shape[-1]
    m = bb * ts
    hd = d // N_X_HEADS
    ap = jnp.concatenate([a_ref[...].reshape(m, -1), p_ref[...].reshape(m, -1)], axis=-1)
    x1 = x_ref[...].reshape(m, d) + jnp.dot(ap, wo_ref[...], preferred_element_type=F32)
    h = _rms_bf16(x1, g_ref[...], NORM_EPS)
    q = jnp.dot(h, wq_ref[...], preferred_element_type=F32).astype(BF16)
    units = [(i, hh) for i in range(bb) for hh in range(N_X_HEADS)]
    scores = lambda i, hh: lax.dot_general(q[i * ts:(i + 1) * ts, hh * hd:(hh + 1) * hd],
                                           mk_ref[i, :, hh * hd:(hh + 1) * hd], _NT, preferred_element_type=F32)
    outs = []
    s_next = scores(*units[0])
    for n, (i, hh) in enumerate(units):
        s = s_next
        if n + 1 < len(units):
            s_next = scores(*units[n + 1])
        p = jnp.exp(s - jnp.max(s, axis=-1, keepdims=True))
        inv_l = 1.0 / jnp.sum(p, axis=-1, keepdims=True)
        o = jnp.dot(p.astype(BF16), mv_ref[i, :, hh * hd:(hh + 1) * hd], preferred_element_type=F32) * inv_l
        outs.append(o.astype(BF16))
    rows = [jnp.concatenate(outs[i * N_X_HEADS:(i + 1) * N_X_HEADS], axis=-1) for i in range(bb)]
    o_all = rows[0] if bb == 1 else jnp.concatenate(rows, axis=0)
    o_ref[...] = (x1 + jnp.dot(o_all, wox_ref[...], preferred_element_type=F32)).reshape(bb, ts, d)


def _mix(x, a, pool_y, w_out, g, wq, mk, mv, layer, wo_x):
    batch, seq, d = x.shape
    bb, ts = _row_tiling(batch, seq, WIDE_ROW_TILE if seq >= WIDE_ROW_TILE else None)
    n_mem = mk.shape[2]
    tok = lambda width: pl.BlockSpec((bb, ts, width), lambda b, s: (b, s, 0))
    mem = pl.BlockSpec((None, bb, n_mem, d), lambda b, s: (layer, b, 0, 0))
    return pl.pallas_call(
        functools.partial(_mix_kernel, bb=bb, ts=ts),
        grid=(batch // bb, seq // ts),
        in_specs=[tok(d), tok(a.shape[-1]), tok(pool_y.shape[-1]), _const_spec(w_out.shape), _const_spec(g.shape),
                  _const_spec(wq.shape), mem, mem, _const_spec(wo_x.shape)],
        out_specs=tok(d),
        out_shape=jax.ShapeDtypeStruct(x.shape, F32),
        compiler_params=_params("arbitrary", "arbitrary"),
        name="mix",
    )(x, a, pool_y, w_out, g, wq, mk, mv, wo_x)


def _mlp_kernel(x_ref, g_ref, wu_ref, wd_ref, fg_ref, o_ref, *, bb, ts, final):
    d = x_ref.shape[-1]
    m = bb * ts
    x = x_ref[...].reshape(m, d)
    h = _rms_bf16(x, g_ref[...], NORM_EPS)
    ck = wu_ref.shape[1] // MLP_CHUNKS
    acc = x
    for j in range(MLP_CHUNKS):
        up = jnp.maximum(jnp.dot(h, wu_ref[:, j * ck:(j + 1) * ck], preferred_element_type=F32), 0.0)
        acc = acc + jnp.dot((up * up).astype(BF16), wd_ref[j * ck:(j + 1) * ck, :], preferred_element_type=F32)
    if final:
        ms = jnp.mean(acc * acc, axis=-1, keepdims=True)
        acc = acc * lax.rsqrt(ms + NORM_EPS) * fg_ref[...]
    o_ref[...] = acc.reshape(bb, ts, d)


def _mlp(x, g, w_up, w_down, final_g, final):
    batch, seq, d = x.shape
    bb, ts = _row_tiling(batch, seq, WIDE_ROW_TILE if seq >= WIDE_ROW_TILE else None)
    assert w_up.shape[1] % MLP_CHUNKS == 0
    tok = pl.BlockSpec((bb, ts, d), lambda b, s: (b, s, 0))
    return pl.pallas_call(
        functools.partial(_mlp_kernel, bb=bb, ts=ts, final=final),
        grid=(batch // bb, seq // ts),
        in_specs=[tok, _const_spec(g.shape), _const_spec(w_up.shape), _const_spec(w_down.shape),
                  _const_spec(final_g.shape)],
        out_specs=tok,
        out_shape=jax.ShapeDtypeStruct(x.shape, F32),
        compiler_params=_params("arbitrary", "arbitrary"),
        name="mlp",
    )(x, g, w_up, w_down, final_g)


def _head_major(w):
    d = w.shape[0]
    return w.reshape(d, 2, N_DIFF_HEADS, DIFF_DK).transpose(0, 2, 1, 3).reshape(d, DIFF_WIDTH)


def _in_weights(w_in):
    w = DIFF_WIDTH
    wq, wk, wv, wu = w_in[:, :w], w_in[:, w:2 * w], w_in[:, 2 * w:3 * w], w_in[:, 3 * w:]
    wq = _head_major(wq) * (DIFF_DK ** -0.5 * LOG2E)
    cols_p = jnp.concatenate([wv, wk, wu], axis=1).astype(BF16)
    rows_t = wq.T.astype(BF16)
    cols_s = jnp.concatenate([wq, wv, wk, wu], axis=1).astype(BF16)
    return cols_p, rows_t, cols_s


def kernel(x_prompt, x_sample, cache_k, cache_v, state_pool, cache_mem_k, cache_mem_v, mem_prompt,
           norm_mix_g, w_in, lam_q, lam_k, subln_g, w_pool, pool_scale, w_out,
           norm_x_g, norm_mem_g, wq_x, wk_x, wv_x, wo_x, norm_mlp_g, w_up, w_down, final_g):
    depth = w_in.shape[0]
    batch, seq, d = x_prompt.shape
    dec_batch, dec_seq, _ = x_sample.shape
    past = cache_k.shape[2]
    n_mem = mem_prompt.shape[1]
    hd = d // N_X_HEADS

    ck = cache_k.reshape(depth, dec_batch, past, DIFF_WIDTH)
    cv = cache_v.reshape(depth, dec_batch, past, DIFF_WIDTH)
    cmk = cache_mem_k.reshape(depth, dec_batch, n_mem, d).astype(BF16)
    cmv = cache_mem_v.reshape(depth, dec_batch, n_mem, d).astype(BF16)
    hist_s = jnp.pad(state_pool, ((0, 0), (0, 0), (HIST_ROWS - POOL_HIST, 0), (0, 0)))
    hist_p = jnp.zeros((batch, HIST_ROWS, POOL_WIDTH), F32)
    row = lambda v: v.reshape(1, -1)
    fg = row(final_g)

    xp, xs = x_prompt, x_sample
    outs = [[] for _ in range(6)]
    kv_prompt = None
    for l in range(depth):
        lam_init = 0.8 - 0.6 * math.exp(-0.3 * l)
        last = l == depth - 1
        w_cols_p, w_rows_t, w_cols_s = _in_weights(w_in[l])
        wp = w_pool[l].astype(BF16)
        ps = row(pool_scale[l])
        g_mix, g_x, g_mlp, sg = row(norm_mix_g[l]), row(norm_x_g[l]), row(norm_mlp_g[l]), row(subln_g[l])
        w_o = w_out[l].astype(BF16)
        wq = (wq_x[l] * hd ** -0.5).astype(BF16)
        wo = wo_x[l].astype(BF16)
        wu, wd = w_up[l].astype(BF16), w_down[l].astype(BF16)
        w_kv = jnp.concatenate([wk_x[l], wv_x[l]], axis=1).astype(BF16)

        qt, vta, ka, *kv_prompt, py, ph = _in_proj_prompt(xp, g_mix, w_cols_p, w_rows_t, hist_p, wp, ps, l, depth,
                                                          kv_prompt)
        a = _prompt_attn(qt, ka, vta, lam_q[l], lam_k[l], sg.reshape(-1, 1), lam_init)
        mk, mv, mkb, mvb = _mem_kv(mem_prompt, row(norm_mem_g[l]), w_kv)
        xp = _mix(xp, a, py, w_o, g_x, wq, mkb[None], mvb[None], 0, wo)
        xp = _mlp(xp, g_mlp, wu, wd, fg, last)
        for dst, val in zip(outs[:3], (ph[:, HIST_ROWS - POOL_HIST:], mk, mv)):
            dst.append(val)

        q, ko, vo, py, ph = _in_proj_sample(xs, g_mix, w_cols_s, hist_s[l], wp, ps, past)
        a = _sample_attn(q, ko, vo, ck, cv, l, lam_q[l], lam_k[l], sg, lam_init)
        xs = _mix(xs, a, py, w_o, g_x, wq, cmk, cmv, l, wo)
        xs = _mlp(xs, g_mlp, wu, wd, fg, last)
        for dst, val in zip(outs[3:], (ko, vo, ph[:, HIST_ROWS - POOL_HIST:])):
            dst.append(val)

    kp, vp = kv_prompt
    pp, mkp, mvp, ks, vs, pps = [jnp.stack(o) for o in outs]
    return (xp, xs,
            kp.reshape(depth, batch, seq, 2, N_DIFF_HEADS, DIFF_DK),
            vp.reshape(depth, batch, seq, N_DIFF_HEADS, DIFF_DV),
            pp,
            mkp.reshape(depth, batch, n_mem, N_X_HEADS, hd),
            mvp.reshape(depth, batch, n_mem, N_X_HEADS, hd),
            ks.reshape(depth, dec_batch, dec_seq, 2, N_DIFF_HEADS, DIFF_DK),
            vs.reshape(depth, dec_batch, dec_seq, N_DIFF_HEADS, DIFF_DV),
            pps)
```

```python
import functools
import math

import jax
import jax.numpy as jnp
from jax import lax
from jax.experimental import pallas as pl
from jax.experimental.pallas import tpu as pltpu

F32 = jnp.float32
BF16 = jnp.bfloat16

CHUNK = 64
CHUNK_SHIFT = CHUNK.bit_length() - 1
N_DIFF_HEADS = 4
DIFF_DK = 64
DIFF_DV = 2 * DIFF_DK
DIFF_WIDTH = N_DIFF_HEADS * DIFF_DV
V_AUG_ROWS = DIFF_DV + 16
POOL_WINDOWS = (2, 4, 8, 16)
POOL_GC = 128
POOL_WIDTH = len(POOL_WINDOWS) * POOL_GC
assert all(w & (w - 1) == 0 for w in POOL_WINDOWS)
POOL_HIST = max(POOL_WINDOWS) - 1
HIST_ROWS = 16
N_X_HEADS = 4
NORM_EPS = 1e-6
SUBLN_EPS = 1e-5
LOG2E = 1.4426950408889634

ROW_TILE = 512
WIDE_ROW_TILE = 1024
MLP_CHUNKS = 4
V7X_VMEM_LIMIT = 56 * 2**20

_NT = (((1,), (1,)), ((), ()))


def _rms_bf16(x, g, eps):
    ms = jnp.mean(x * x, axis=-1, keepdims=True)
    return (x * lax.rsqrt(ms + eps) * g).astype(BF16)


def _row_tiling(batch, seq, rows=None):
    rows = ROW_TILE if rows is None else rows
    if seq >= rows:
        assert seq % rows == 0
        return 1, rows
    bb = max(1, min(batch, rows // seq))
    assert batch % bb == 0 and seq % 8 == 0
    return bb, seq


def _params(*semantics):
    return pltpu.CompilerParams(dimension_semantics=semantics, vmem_limit_bytes=V7X_VMEM_LIMIT)


def _const_spec(shape):
    nd = len(shape)
    return pl.BlockSpec(shape, lambda *_: (0,) * nd, pipeline_mode=pl.Buffered(1))


def _project(x_ref, g_ref, w_ref, ko_ref, vo_ref, ext_ref, hist_ref, ph_ref, wp_ref, ps_ref, py_ref, bb, ts, start):
    d = x_ref.shape[-1]
    m = bb * ts
    w = DIFF_WIDTH
    h = _rms_bf16(x_ref[...].reshape(m, d), g_ref[...], NORM_EPS)
    u = jnp.dot(h, w_ref[:, w_ref.shape[1] - POOL_WIDTH:], preferred_element_type=F32)

    @pl.when(pl.program_id(1) == 0)
    def _():
        ext_ref[:, 0:HIST_ROWS, :] = hist_ref[...]

    ext_ref[:, HIST_ROWS:HIST_ROWS + ts, :] = u.reshape(bb, ts, POOL_WIDTH)
    groups = _pool_windows(ext_ref, ph_ref, bb, ts, start)
    half = len(groups) // 2
    n_lead = w_ref.shape[1] - 2 * w - POOL_WIDTH
    first = jnp.dot(h, w_ref[:, :n_lead], preferred_element_type=F32) if n_lead else None
    _pool_project(groups[:half], 0, wp_ref, ps_ref, py_ref, bb, ts)
    v = jnp.dot(h, w_ref[:, n_lead:n_lead + w], preferred_element_type=F32)
    vo_ref[...] = v.reshape(bb, ts, w)
    _pool_project(groups[half:], half, wp_ref, ps_ref, py_ref, bb, ts)
    k = jnp.dot(h, w_ref[:, n_lead + w:n_lead + 2 * w], preferred_element_type=F32)
    ko_ref[...] = k.reshape(bb, ts, w)
    return h, first, v, k


def _pool_windows(ext_ref, ph_ref, bb, ts, start):
    pos = start + pl.program_id(1) * ts + lax.broadcasted_iota(jnp.int32, (ts, 1), 0)
    groups = []
    for gi, win_len in enumerate(POOL_WINDOWS):
        cols = slice(gi * POOL_GC, (gi + 1) * POOL_GC)
        inv_cnt = 1.0 / jnp.minimum(win_len, pos + 1).astype(F32)
        pooled = []
        for i in range(bb):
            win = ext_ref[i, :, cols]
            k = 1
            while k < win_len:
                win = win + pltpu.roll(win, k, axis=0)
                k *= 2
            u = ext_ref[i, HIST_ROWS:HIST_ROWS + ts, cols]
            pooled.append((win[HIST_ROWS:] * inv_cnt - u).astype(BF16))
        groups.append(pooled[0] if bb == 1 else jnp.concatenate(pooled, axis=0))
    tail = ext_ref[:, ts:ts + HIST_ROWS, :]
    ph_ref[...] = tail
    ext_ref[:, 0:HIST_ROWS, :] = tail
    return groups


def _pool_project(groups, first_group, wp_ref, ps_ref, py_ref, bb, ts):
    for gi, pooled in enumerate(groups, first_group):
        cols = slice(gi * POOL_GC, (gi + 1) * POOL_GC)
        y = jnp.dot(pooled, wp_ref[gi], preferred_element_type=F32) * ps_ref[:, cols]
        py_ref[:, :, cols] = y.astype(BF16).reshape(bb, ts, POOL_GC)


def _in_proj_prompt_kernel(x_ref, g_ref, w_ref, wt_ref, hist_ref, wp_ref, ps_ref, *refs, ts, start):
    qt_ref, vta_ref, ka_ref, ko_ref, vo_ref, py_ref, ph_ref, ext_ref = refs[-8:]
    if len(refs) == 8:
        for ref in (ko_ref, vo_ref):
            if ref.shape[0] > 1:
                ref[1:] = jnp.zeros((ref.shape[0] - 1,) + ref.shape[1:], F32)
        ko_ref, vo_ref = ko_ref.at[0], vo_ref.at[0]
    h, _, v, k = _project(x_ref, g_ref, w_ref, ko_ref, vo_ref, ext_ref, hist_ref, ph_ref, wp_ref, ps_ref, py_ref,
                       1, ts, start)
    blocks = [k[:, j * DIFF_DV:(j + 1) * DIFF_DV] for j in range(DIFF_WIDTH // DIFF_DV)]
    low_half = lax.broadcasted_iota(jnp.int32, (ts, DIFF_DV), 1) < DIFF_DK
    for hd in range(N_DIFF_HEADS):
        map0, map1 = blocks[hd // 2], blocks[N_DIFF_HEADS // 2 + hd // 2]
        if hd % 2 == 0:
            k_hm = jnp.where(low_half, map0, pltpu.roll(map1, DIFF_DK, axis=1))
        else:
            k_hm = jnp.where(low_half, pltpu.roll(map0, DIFF_DK, axis=1), map1)
        ka_ref[:, hd * DIFF_DV:(hd + 1) * DIFF_DV] = k_hm.astype(BF16)
    qt_ref[...] = lax.dot_general(wt_ref[...], h, _NT, preferred_element_type=F32).astype(BF16)
    ones = (lax.broadcasted_iota(jnp.int32, (V_AUG_ROWS - DIFF_DV, ts), 0) == 0).astype(BF16)
    for hd in range(N_DIFF_HEADS):
        vta_ref[hd * V_AUG_ROWS:hd * V_AUG_ROWS + DIFF_DV, :] = v[:, hd * DIFF_DV:(hd + 1) * DIFF_DV].T.astype(BF16)
        vta_ref[hd * V_AUG_ROWS + DIFF_DV:(hd + 1) * V_AUG_ROWS, :] = ones


def _in_proj_sample_kernel(x_ref, g_ref, w_ref, hist_ref, wp_ref, ps_ref,
                           q_ref, ko_ref, vo_ref, py_ref, ph_ref, ext_ref, *, bb, ts, start):
    _, q, _, _ = _project(x_ref, g_ref, w_ref, ko_ref, vo_ref, ext_ref, hist_ref, ph_ref, wp_ref, ps_ref, py_ref,
                       bb, ts, start)
    q_ref[...] = q.astype(BF16).reshape(bb, ts, DIFF_WIDTH)


def _in_proj_common(batch, seq, bb, ts):
    assert seq >= HIST_ROWS
    tok = lambda width: pl.BlockSpec((bb, ts, width), lambda b, s: (b, s, 0))
    hist_spec = pl.BlockSpec((bb, HIST_ROWS, POOL_WIDTH), lambda b, s: (b, 0, 0))
    tok_shape = lambda width, dt: jax.ShapeDtypeStruct((batch, seq, width), dt)
    pool_specs = [tok(POOL_WIDTH), hist_spec]
    pool_shapes = [tok_shape(POOL_WIDTH, BF16), jax.ShapeDtypeStruct((batch, HIST_ROWS, POOL_WIDTH), F32)]
    return tok, hist_spec, tok_shape, pool_specs, pool_shapes


def _in_proj_prompt(x, g, w_cols, w_rows_t, hist, w_pool, pool_scale, layer, depth, kv_all):
    batch, seq, d = x.shape
    bb, ts = _row_tiling(batch, seq)
    assert bb == 1
    w = DIFF_WIDTH
    tok, hist_spec, tok_shape, pool_specs, pool_shapes = _in_proj_common(batch, seq, bb, ts)
    va_rows = N_DIFF_HEADS * V_AUG_ROWS
    tile_t = lambda rows: pl.BlockSpec((None, None, rows, ts), lambda b, s: (b, s, 0, 0))
    tile_t_shape = lambda rows: jax.ShapeDtypeStruct((batch, seq // ts, rows, ts), BF16)
    if kv_all is None:
        assert layer == 0
        layered = pl.BlockSpec((depth, bb, ts, w), lambda b, s: (0, b, s, 0))
    else:
        layered = pl.BlockSpec((None, bb, ts, w), lambda b, s: (layer, b, s, 0))
    in_specs = [tok(d), _const_spec(g.shape), _const_spec(w_cols.shape), _const_spec(w_rows_t.shape),
                hist_spec, _const_spec(w_pool.shape), _const_spec(pool_scale.shape)]
    args = [x, g, w_cols, w_rows_t, hist, w_pool, pool_scale]
    aliases = {}
    if kv_all is not None:
        aliases = {len(args): 3, len(args) + 1: 4}
        in_specs += [pl.BlockSpec(memory_space=pl.ANY)] * 2
        args += list(kv_all)
    return pl.pallas_call(
        functools.partial(_in_proj_prompt_kernel, ts=ts, start=0),
        grid=(batch // bb, seq // ts),
        in_specs=in_specs,
        out_specs=[tile_t(w), tile_t(va_rows),
                   pl.BlockSpec((None, ts, w), lambda b, s: (b, s, 0)), layered, layered] + pool_specs,
        out_shape=[tile_t_shape(w), tile_t_shape(va_rows), tok_shape(w, BF16),
                   jax.ShapeDtypeStruct((depth, batch, seq, w), F32),
                   jax.ShapeDtypeStruct((depth, batch, seq, w), F32)] + pool_shapes,
        input_output_aliases=aliases,
        scratch_shapes=[pltpu.VMEM((bb, HIST_ROWS + ts, POOL_WIDTH), F32)],
        compiler_params=_params("arbitrary", "arbitrary"),
        name="in_proj_prompt",
    )(*args)


def _in_proj_sample(x, g, w_cols, hist, w_pool, pool_scale, start):
    batch, seq, d = x.shape
    bb, ts = _row_tiling(batch, seq)
    w = DIFF_WIDTH
    tok, hist_spec, tok_shape, pool_specs, pool_shapes = _in_proj_common(batch, seq, bb, ts)
    return pl.pallas_call(
        functools.partial(_in_proj_sample_kernel, bb=bb, ts=ts, start=start),
        grid=(batch // bb, seq // ts),
        in_specs=[tok(d), _const_spec(g.shape), _const_spec(w_cols.shape),
                  hist_spec, _const_spec(w_pool.shape), _const_spec(pool_scale.shape)],
        out_specs=[tok(w), tok(w), tok(w)] + pool_specs,
        out_shape=[tok_shape(w, BF16), tok_shape(w, F32), tok_shape(w, F32)] + pool_shapes,
        scratch_shapes=[pltpu.VMEM((bb, HIST_ROWS + ts, POOL_WIDTH), F32)],
        compiler_params=_params("arbitrary", "arbitrary"),
        name="in_proj_sample",
    )(x, g, w_cols, hist, w_pool, pool_scale)


def _lambda(lq_ref, lk_ref, lam_init):
    e = jnp.exp(jnp.sum(lq_ref[...] * lk_ref[...], axis=-1, keepdims=True))
    return e[0:1] - e[1:2] + lam_init


def _prompt_attn_kernel(qt_ref, k_ref, vt_ref, lq_ref, lk_ref, sg_ref, o_ref, s_ref, acc_ref, *, t, n_blk, lam_init):
    lam = _lambda(lq_ref, lk_ref, lam_init)

    def masked_q(i, mp):
        zero = jnp.zeros((DIFF_DK, t), BF16)
        if mp == 0:
            return jnp.concatenate([qt_ref[i, :DIFF_DK, :], zero], axis=0)
        return jnp.concatenate([zero, qt_ref[i, DIFF_DK:, :]], axis=0)

    def scores(i, j, slot, mp):
        k_blk = k_ref[pl.ds(pl.multiple_of(j * t, t), t), :]
        s_ref[slot, mp] = jnp.dot(k_blk, masked_q(i, mp), preferred_element_type=F32)

    def update(j, slot, mp, m, masked):
        s = s_ref[slot, mp]
        if masked:
            key = lax.broadcasted_iota(jnp.int32, s.shape, 0)
            qry = lax.broadcasted_iota(jnp.int32, s.shape, 1)
            s = jnp.where(lax.shift_right_logical(key, CHUNK_SHIFT) <= lax.shift_right_logical(qry, CHUNK_SHIFT),
                          s, -jnp.inf)
        m_new = jnp.maximum(m, jnp.max(s, axis=0, keepdims=True))
        alpha = jnp.exp2(m - m_new)
        p = jnp.exp2(s - m_new)
        acc_ref[mp] = alpha * acc_ref[mp] + jnp.dot(vt_ref[j], p.astype(BF16), preferred_element_type=F32)
        return m_new

    def chunk_masked(s, key0, qry0):
        key = lax.broadcasted_iota(jnp.int32, s.shape, 0) + key0
        qry = lax.broadcasted_iota(jnp.int32, s.shape, 1) + qry0
        return jnp.where(lax.shift_right_logical(key, CHUNK_SHIFT) <= lax.shift_right_logical(qry, CHUNK_SHIFT),
                         s, -jnp.inf)

    def update_diagonal(i, slot, mp, m):
        hf = t // 2
        if hf % DIFF_DV or hf % CHUNK:
            return update(i, slot, mp, m, True)
        s_top = chunk_masked(s_ref[slot, mp, :hf, :], 0, 0)
        s_bot = chunk_masked(s_ref[slot, mp, hf:, hf:], hf, hf)
        m_top = jnp.max(s_top, axis=0, keepdims=True)
        m_blk = jnp.concatenate([m_top[:, :hf], jnp.maximum(m_top[:, hf:], jnp.max(s_bot, axis=0, keepdims=True))],
                                axis=1)
        m_new = jnp.maximum(m, m_blk)
        alpha = jnp.exp2(m - m_new)
        p_top = jnp.exp2(s_top - m_new).astype(BF16)
        p_bot = jnp.exp2(s_bot - m_new[:, hf:]).astype(BF16)
        acc_ref[mp] = alpha * acc_ref[mp] + jnp.dot(vt_ref[i, :, :hf], p_top, preferred_element_type=F32)
        acc_ref[mp, :, hf:] = acc_ref[mp, :, hf:] + jnp.dot(vt_ref[i, :, hf:], p_bot, preferred_element_type=F32)
        return m_new

    def step(i, j, cur, nxt, m0, m1):
        scores(i, j + 1, nxt, 0)
        m0 = update(j, cur, 0, m0, False)
        scores(i, j + 1, nxt, 1)
        m1 = update(j, cur, 1, m1, False)
        return m0, m1

    def finish(i, slot, m0, m1, next_first):
        if next_first is not None:
            scores(i + 1, 0, next_first, 0)
        update_diagonal(i, slot, 0, m0)
        if next_first is not None:
            scores(i + 1, 0, next_first, 1)
        update_diagonal(i, slot, 1, m1)
        normalised = lambda acc: acc[:DIFF_DV] * (1.0 / acc[DIFF_DV:DIFF_DV + 1])
        o = normalised(acc_ref[0]) - lam * normalised(acc_ref[1])
        ms = jnp.mean(o * o, axis=0, keepdims=True)
        y = o * lax.rsqrt(ms + SUBLN_EPS) * sg_ref[...] * (1.0 - lam_init)
        o_ref[i] = y.astype(BF16)

    def start():
        acc_ref[...] = jnp.zeros(acc_ref.shape, F32)
        return jnp.full((1, t), -jnp.inf, F32)

    first_slot = lambda i_mod_2: 3 if i_mod_2 else 2

    def query_block(i, rem, last):
        init = start()
        m = step(i, 0, first_slot((rem + 1) % 2), 0, init, init)
        pair = lambda j, c: step(i, j + 1, 1, 0, *step(i, j, 0, 1, *c))
        n_quads = (i - 1) // 4
        m = lax.fori_loop(0, n_quads, lambda p, c: pair(4 * p + 3, pair(4 * p + 1, c)), m)
        for n in range(rem):
            m = step(i, 4 * n_quads + 1 + n, n % 2, 1 - n % 2, *m)
        finish(i, rem % 2, *m, None if last else first_slot(rem % 2))

    scores(0, 0, 2, 0)
    scores(0, 0, 2, 1)
    init = start()
    finish(0, 2, init, init, first_slot(1) if n_blk > 1 else None)
    n_tail = (n_blk - 2) % 4 + 1 if n_blk > 1 else 0
    n_groups = (n_blk - 1 - n_tail) // 4

    def four_query_blocks(g, carry):
        for rem in range(4):
            query_block(4 * g + 1 + rem, rem, False)
        return carry

    lax.fori_loop(0, n_groups, four_query_blocks, 0)
    for i in range(n_blk - n_tail, n_blk):
        query_block(i, (i - 1) % 4, i == n_blk - 1)


def _prompt_attn(qt, k, vta, lam_q, lam_k, subln_g_col, lam_init):
    batch, n_blk, _, t = qt.shape
    seq = n_blk * t
    assert t % CHUNK == 0 and (t & (t - 1)) == 0
    return pl.pallas_call(
        functools.partial(_prompt_attn_kernel, t=t, n_blk=n_blk, lam_init=lam_init),
        grid=(batch, N_DIFF_HEADS),
        in_specs=[pl.BlockSpec((None, n_blk, DIFF_DV, t), lambda b, h: (b, 0, h, 0)),
                  pl.BlockSpec((None, seq, DIFF_DV), lambda b, h: (b, 0, h)),
                  pl.BlockSpec((None, n_blk, V_AUG_ROWS, t), lambda b, h: (b, 0, h, 0)),
                  _const_spec(lam_q.shape), _const_spec(lam_k.shape), _const_spec(subln_g_col.shape)],
        out_specs=pl.BlockSpec((None, n_blk, DIFF_DV, t), lambda b, h: (b, 0, h, 0)),
        out_shape=jax.ShapeDtypeStruct((batch, n_blk, DIFF_WIDTH, t), BF16),
        scratch_shapes=[pltpu.VMEM((4, 2, t, t), F32),
                        pltpu.VMEM((2, V_AUG_ROWS, t), F32)],
        compiler_params=_params("arbitrary", "arbitrary"),
        name="prompt_attn",
    )(qt, k, vta, lam_q, lam_k, subln_g_col)


def _sample_attn_kernel(q_ref, kn_ref, vn_ref, ck_ref, cv_ref, lq_ref, lk_ref, sg_ref, o_ref,
                        *, past, n_new, lam_init):
    lam = _lambda(lq_ref, lk_ref, lam_init)
    sg = sg_ref[...]
    lane = lax.broadcasted_iota(jnp.int32, (n_new, DIFF_DV), 1)
    r = lax.broadcasted_iota(jnp.int32, (n_new, n_new), 0)
    c = lax.broadcasted_iota(jnp.int32, (n_new, n_new), 1)
    visible = lax.shift_right_logical(past + c, CHUNK_SHIFT) <= lax.shift_right_logical(past + r, CHUNK_SHIFT)
    visible2 = jnp.concatenate([visible, visible], axis=0)
    head_cols = lambda h: slice(h * DIFF_DV, (h + 1) * DIFF_DV)

    def masked_q(h, mp):
        q_blk = q_ref[:, head_cols(h)].astype(F32)
        qm = q_blk if mp == h % 2 else pltpu.roll(q_blk, DIFF_DK, axis=1)
        return jnp.where((lane >= DIFF_DK) == bool(h % 2), qm, 0.0).astype(BF16)

    v_c = [cv_ref[:, head_cols(h)].astype(BF16) for h in range(N_DIFF_HEADS)]
    v_n = [vn_ref[:, head_cols(h)].astype(BF16) for h in range(N_DIFF_HEADS)]
    o_maps = {}
    for mp in range(2):
        for pair in range(N_DIFF_HEADS // 2):
            heads = (2 * pair, 2 * pair + 1)
            ks = head_cols(2 * mp + pair)
            qq = jnp.concatenate([masked_q(h, mp) for h in heads], axis=0)
            s_c = lax.dot_general(qq, ck_ref[:, ks].astype(BF16), _NT, preferred_element_type=F32)
            s_n = lax.dot_general(qq, kn_ref[:, ks].astype(BF16), _NT, preferred_element_type=F32)
            s_n = jnp.where(visible2, s_n, -jnp.inf)
            mx = jnp.maximum(jnp.max(s_c, axis=-1, keepdims=True), jnp.max(s_n, axis=-1, keepdims=True))
            p_c = jnp.exp2(s_c - mx)
            p_n = jnp.exp2(s_n - mx)
            inv_l = 1.0 / (jnp.sum(p_c, axis=-1, keepdims=True) + jnp.sum(p_n, axis=-1, keepdims=True))
            p_c, p_n = p_c.astype(BF16), p_n.astype(BF16)
            for n, h in enumerate(heads):
                rows = slice(n * n_new, (n + 1) * n_new)
                o = jnp.dot(p_c[rows], v_c[h], preferred_element_type=F32) + \
                    jnp.dot(p_n[rows], v_n[h], preferred_element_type=F32)
                o_maps[h, mp] = o * inv_l[rows]
    for h in range(N_DIFF_HEADS):
        o = o_maps[h, 0] - lam * o_maps[h, 1]
        ms = jnp.mean(o * o, axis=-1, keepdims=True)
        o_ref[:, head_cols(h)] = (o * lax.rsqrt(ms + SUBLN_EPS) * sg * (1.0 - lam_init)).astype(BF16)


def _sample_attn(q, k_new, v_new, cache_k, cache_v, layer, lam_q, lam_k, subln_g, lam_init):
    batch, n_new, _ = q.shape
    past = cache_k.shape[2]
    w = DIFF_WIDTH
    new = pl.BlockSpec((None, n_new, w), lambda b: (b, 0, 0))
    cached = pl.BlockSpec((None, None, past, w), lambda b: (layer, b, 0, 0))
    return pl.pallas_call(
        functools.partial(_sample_attn_kernel, past=past, n_new=n_new, lam_init=lam_init),
        grid=(batch,),
        in_specs=[new, new, new, cached, cached,
                  _const_spec(lam_q.shape), _const_spec(lam_k.shape), _const_spec(subln_g.shape)],
        out_specs=new,
        out_shape=jax.ShapeDtypeStruct((batch, n_new, w), BF16),
        compiler_params=_params("arbitrary"),
        name="sample_attn",
    )(q, k_new, v_new, cache_k, cache_v, lam_q, lam_k, subln_g)


def _mem_kv_kernel(m_ref, g_ref, w_ref, k_ref, v_ref, kb_ref, vb_ref):
    d = m_ref.shape[-1]
    z = jnp.dot(_rms_bf16(m_ref[...], g_ref[...], NORM_EPS), w_ref[...], preferred_element_type=F32)
    k_ref[...] = z[:, :d]
    v_ref[...] = z[:, d:]
    kb_ref[...] = z[:, :d].astype(BF16)
    vb_ref[...] = z[:, d:].astype(BF16)


def _mem_kv(mem, g, w_kv):
    batch, n_mem, d = mem.shape
    rows = batch * n_mem
    tm = min(ROW_TILE, rows)
    assert rows % tm == 0
    tok = pl.BlockSpec((tm, d), lambda i: (i, 0))
    outs = pl.pallas_call(
        _mem_kv_kernel,
        grid=(rows // tm,),
        in_specs=[tok, _const_spec(g.shape), _const_spec(w_kv.shape)],
        out_specs=[tok] * 4,
        out_shape=[jax.ShapeDtypeStruct((rows, d), F32)] * 2 + [jax.ShapeDtypeStruct((rows, d), BF16)] * 2,
        compiler_params=_params("arbitrary"),
        name="mem_kv",
    )(mem.reshape(rows, d), g, w_kv)
    return [o.reshape(batch, n_mem, d) for o in outs]


def _mix_kernel(x_ref, a_ref, p_ref, wo_ref, g_ref, wq_ref, mk_ref, mv_ref, wox_ref, o_ref, *, bb, ts, a_transposed):
    d = x_ref.shape[-1]
    m = bb * ts
    hd = d // N_X_HEADS
    if not a_transposed:
        a = a_ref[...].reshape(m, -1)
    else:
        tiles = [a_ref[j].astype(F32).T.astype(BF16) for j in range(a_ref.shape[0])]
        a = tiles[0] if len(tiles) == 1 else jnp.concatenate(tiles, axis=0)
    ap = jnp.concatenate([a, p_ref[...].reshape(m, -1)], axis=-1)
    x1 = x_ref[...].reshape(m, d) + jnp.dot(ap, wo_ref[...], preferred_element_type=F32)
    h = _rms_bf16(x1, g_ref[...], NORM_EPS)
    q = jnp.dot(h, wq_ref[...], preferred_element_type=F32).astype(BF16)
    units = [(i, hh) for i in range(bb) for hh in range(N_X_HEADS)]
    scores = lambda i, hh: lax.dot_general(q[i * ts:(i + 1) * ts, hh * hd:(hh + 1) * hd],
                                           mk_ref[i, :, hh * hd:(hh + 1) * hd], _NT, preferred_element_type=F32)
    outs = []
    s_next = scores(*units[0])
    for n, (i, hh) in enumerate(units):
        s = s_next
        if n + 1 < len(units):
            s_next = scores(*units[n + 1])
        p = jnp.exp(s - jnp.max(s, axis=-1, keepdims=True))
        inv_l = 1.0 / jnp.sum(p, axis=-1, keepdims=True)
        o = jnp.dot(p.astype(BF16), mv_ref[i, :, hh * hd:(hh + 1) * hd], preferred_element_type=F32) * inv_l
        outs.append(o.astype(BF16))
    rows = [jnp.concatenate(outs[i * N_X_HEADS:(i + 1) * N_X_HEADS], axis=-1) for i in range(bb)]
    o_all = rows[0] if bb == 1 else jnp.concatenate(rows, axis=0)
    o_ref[...] = (x1 + jnp.dot(o_all, wox_ref[...], preferred_element_type=F32)).reshape(bb, ts, d)


def _mix(x, a, pool_y, w_out, g, wq, mk, mv, layer, wo_x):
    batch, seq, d = x.shape
    bb, ts = _row_tiling(batch, seq, WIDE_ROW_TILE if seq >= WIDE_ROW_TILE else None)
    n_mem = mk.shape[2]
    tok = lambda width: pl.BlockSpec((bb, ts, width), lambda b, s: (b, s, 0))
    mem = pl.BlockSpec((None, bb, n_mem, d), lambda b, s: (layer, b, 0, 0))
    a_transposed = a.ndim == 4
    if a_transposed:
        t = a.shape[-1]
        assert bb == 1 and ts % t == 0
        a_spec = pl.BlockSpec((None, ts // t, a.shape[2], t), lambda b, s: (b, s, 0, 0))
    else:
        a_spec = tok(a.shape[-1])
    return pl.pallas_call(
        functools.partial(_mix_kernel, bb=bb, ts=ts, a_transposed=a_transposed),
        grid=(batch // bb, seq // ts),
        in_specs=[tok(d), a_spec, tok(pool_y.shape[-1]), _const_spec(w_out.shape), _const_spec(g.shape),
                  _const_spec(wq.shape), mem, mem, _const_spec(wo_x.shape)],
        out_specs=tok(d),
        out_shape=jax.ShapeDtypeStruct(x.shape, F32),
        compiler_params=_params("arbitrary", "arbitrary"),
        name="mix",
    )(x, a, pool_y, w_out, g, wq, mk, mv, wo_x)


def _mlp_kernel(x_ref, g_ref, wu_ref, wd_ref, fg_ref, o_ref, *, bb, ts, final):
    d = x_ref.shape[-1]
    m = bb * ts
    x = x_ref[...].reshape(m, d)
    h = _rms_bf16(x, g_ref[...], NORM_EPS)
    ck = wu_ref.shape[1] // MLP_CHUNKS
    acc = x
    for j in range(MLP_CHUNKS):
        up = jnp.maximum(jnp.dot(h, wu_ref[:, j * ck:(j + 1) * ck], preferred_element_type=F32), 0.0)
        acc = acc + jnp.dot((up * up).astype(BF16), wd_ref[j * ck:(j + 1) * ck, :], preferred_element_type=F32)
    if final:
        ms = jnp.mean(acc * acc, axis=-1, keepdims=True)
        acc = acc * lax.rsqrt(ms + NORM_EPS) * fg_ref[...]
    o_ref[...] = acc.reshape(bb, ts, d)


def _mlp(x, g, w_up, w_down, final_g, final):
    batch, seq, d = x.shape
    bb, ts = _row_tiling(batch, seq, WIDE_ROW_TILE if seq >= WIDE_ROW_TILE else None)
    assert w_up.shape[1] % MLP_CHUNKS == 0
    tok = pl.BlockSpec((bb, ts, d), lambda b, s: (b, s, 0))
    return pl.pallas_call(
        functools.partial(_mlp_kernel, bb=bb, ts=ts, final=final),
        grid=(batch // bb, seq // ts),
        in_specs=[tok, _const_spec(g.shape), _const_spec(w_up.shape), _const_spec(w_down.shape),
                  _const_spec(final_g.shape)],
        out_specs=tok,
        out_shape=jax.ShapeDtypeStruct(x.shape, F32),
        compiler_params=_params("arbitrary", "arbitrary"),
        name="mlp",
    )(x, g, w_up, w_down, final_g)


def _head_major(w):
    d = w.shape[0]
    return w.reshape(d, 2, N_DIFF_HEADS, DIFF_DK).transpose(0, 2, 1, 3).reshape(d, DIFF_WIDTH)


def _in_weights(w_in):
    w = DIFF_WIDTH
    wq, wk, wv, wu = w_in[:, :w], w_in[:, w:2 * w], w_in[:, 2 * w:3 * w], w_in[:, 3 * w:]
    wq = _head_major(wq) * (DIFF_DK ** -0.5 * LOG2E)
    cols_p = jnp.concatenate([wv, wk, wu], axis=1).astype(BF16)
    rows_t = wq.T.astype(BF16)
    cols_s = jnp.concatenate([wq, wv, wk, wu], axis=1).astype(BF16)
    return cols_p, rows_t, cols_s


def kernel(x_prompt, x_sample, cache_k, cache_v, state_pool, cache_mem_k, cache_mem_v, mem_prompt,
           norm_mix_g, w_in, lam_q, lam_k, subln_g, w_pool, pool_scale, w_out,
           norm_x_g, norm_mem_g, wq_x, wk_x, wv_x, wo_x, norm_mlp_g, w_up, w_down, final_g):
    depth = w_in.shape[0]
    batch, seq, d = x_prompt.shape
    dec_batch, dec_seq, _ = x_sample.shape
    past = cache_k.shape[2]
    n_mem = mem_prompt.shape[1]
    hd = d // N_X_HEADS

    ck = cache_k.reshape(depth, dec_batch, past, DIFF_WIDTH)
    cv = cache_v.reshape(depth, dec_batch, past, DIFF_WIDTH)
    cmk = cache_mem_k.reshape(depth, dec_batch, n_mem, d).astype(BF16)
    cmv = cache_mem_v.reshape(depth, dec_batch, n_mem, d).astype(BF16)
    hist_s = jnp.pad(state_pool, ((0, 0), (0, 0), (HIST_ROWS - POOL_HIST, 0), (0, 0)))
    hist_p = jnp.zeros((batch, HIST_ROWS, POOL_WIDTH), F32)
    row = lambda v: v.reshape(1, -1)
    fg = row(final_g)

    xp, xs = x_prompt, x_sample
    outs = [[] for _ in range(6)]
    kv_prompt = None
    for l in range(depth):
        lam_init = 0.8 - 0.6 * math.exp(-0.3 * l)
        last = l == depth - 1
        w_cols_p, w_rows_t, w_cols_s = _in_weights(w_in[l])
        wp = w_pool[l].astype(BF16)
        ps = row(pool_scale[l])
        g_mix, g_x, g_mlp, sg = row(norm_mix_g[l]), row(norm_x_g[l]), row(norm_mlp_g[l]), row(subln_g[l])
        w_o = w_out[l].astype(BF16)
        wq = (wq_x[l] * hd ** -0.5).astype(BF16)
        wo = wo_x[l].astype(BF16)
        wu, wd = w_up[l].astype(BF16), w_down[l].astype(BF16)
        w_kv = jnp.concatenate([wk_x[l], wv_x[l]], axis=1).astype(BF16)

        qt, vta, ka, *kv_prompt, py, ph = _in_proj_prompt(xp, g_mix, w_cols_p, w_rows_t, hist_p, wp, ps, l, depth,
                                                          kv_prompt)
        a = _prompt_attn(qt, ka, vta, lam_q[l], lam_k[l], sg.reshape(-1, 1), lam_init)
        mk, mv, mkb, mvb = _mem_kv(mem_prompt, row(norm_mem_g[l]), w_kv)
        xp = _mix(xp, a, py, w_o, g_x, wq, mkb[None], mvb[None], 0, wo)
        xp = _mlp(xp, g_mlp, wu, wd, fg, last)
        for dst, val in zip(outs[:3], (ph[:, HIST_ROWS - POOL_HIST:], mk, mv)):
            dst.append(val)

        q, ko, vo, py, ph = _in_proj_sample(xs, g_mix, w_cols_s, hist_s[l], wp, ps, past)
        a = _sample_attn(q, ko, vo, ck, cv, l, lam_q[l], lam_k[l], sg, lam_init)
        xs = _mix(xs, a, py, w_o, g_x, wq, cmk, cmv, l, wo)
        xs = _mlp(xs, g_mlp, wu, wd, fg, last)
        for dst, val in zip(outs[3:], (ko, vo, ph[:, HIST_ROWS - POOL_HIST:])):
            dst.append(val)

    kp, vp = kv_prompt
    pp, mkp, mvp, ks, vs, pps = [jnp.stack(o) for o in outs]
    return (xp, xs,
            kp.reshape(depth, batch, seq, 2, N_DIFF_HEADS, DIFF_DK),
            vp.reshape(depth, batch, seq, N_DIFF_HEADS, DIFF_DV),
            pp,
            mkp.reshape(depth, batch, n_mem, N_X_HEADS, hd),
            mvp.reshape(depth, batch, n_mem, N_X_HEADS, hd),
            ks.reshape(depth, dec_batch, dec_seq, 2, N_DIFF_HEADS, DIFF_DK),
            vs.reshape(depth, dec_batch, dec_seq, N_DIFF_HEADS, DIFF_DV),
            pps)
```
